```python
import functools
import jax, jax.numpy as jnp
from jax import lax
import numpy as np

D_MODEL = 4096
BATCH = 4
SEQ = 2048
DEPTH = 2
DEC_BATCH = 8
DEC_SEQ = 8
PAST_LEN = 16384
PAGE_SIZE = 128

N_META = 16
FOX_HEAD_DIM = 128
FOX_HEADS = (D_MODEL // 2) // FOX_HEAD_DIM
FOX_WIDTH = FOX_HEADS * FOX_HEAD_DIM
RWKV_HEAD_DIM = 64
RWKV_HEADS = (D_MODEL // 2) // RWKV_HEAD_DIM
RWKV_WIDTH = RWKV_HEADS * RWKV_HEAD_DIM
DECAY_LORA = max(32, int(round(1.8 * RWKV_WIDTH ** 0.5 / 32)) * 32)
AAA_LORA = max(32, int(round(1.8 * RWKV_WIDTH ** 0.5 / 32)) * 32)
GATE_LORA = max(32, int(round(0.6 * RWKV_WIDTH ** 0.8 / 32)) * 32)
RWKV_PROJ = 3 * RWKV_WIDTH + DECAY_LORA + AAA_LORA + GATE_LORA
FOX_PROJ = 3 * FOX_WIDTH + FOX_HEADS
IN_PROJ = FOX_PROJ + RWKV_PROJ + 2 * D_MODEL
SPLIT_IN = [FOX_WIDTH, 2 * FOX_WIDTH, 3 * FOX_WIDTH, FOX_PROJ, FOX_PROJ + RWKV_PROJ, FOX_PROJ + RWKV_PROJ + D_MODEL]
SPLIT_RWKV = [RWKV_WIDTH, 2 * RWKV_WIDTH, 3 * RWKV_WIDTH, 3 * RWKV_WIDTH + DECAY_LORA,
              3 * RWKV_WIDTH + DECAY_LORA + AAA_LORA]
D_FF = 256 * ((8 * D_MODEL // 3 + 255) // 256)
CONV_WIDTH = 3
Q_BLOCK = 128
NORM_EPS = 1e-6
GN_EPS = 64e-5
NEG_INF = -1e30

kernel_name = 'fox_rwkv7_gated_hybrid_step'


def rms_norm(x, g):
    xf = x.astype(jnp.float32)
    y = xf * lax.rsqrt(jnp.mean(xf * xf, axis=-1, keepdims=True) + NORM_EPS)
    return (y * g.astype(jnp.float32)).astype(x.dtype)


def fox_attend(q, k, v, eq, ek, q_pos, k_pos):
    s = jnp.einsum('bqhd,bkhd->bhqk', q, k).astype(jnp.float32) * (FOX_HEAD_DIM ** -0.5)
    bias = jnp.swapaxes(ek, 1, 2)[:, :, None, :] - jnp.swapaxes(eq, 1, 2)[:, :, :, None]
    mask = k_pos[None, :] <= q_pos[:, None]
    p = jax.nn.softmax(jnp.where(mask, s + bias, NEG_INF), axis=-1)
    return jnp.einsum('bhqk,bkhd->bqhd', p.astype(v.dtype), v)


def fox_prompt(q, k, v, lf):
    B, T, H, Dh = q.shape
    e = lax.cumsum(lf, axis=1, reverse=True) - lf
    pos = jnp.arange(T)
    out_meta = fox_attend(q[:, :N_META], k[:, :N_META], v[:, :N_META], e[:, :N_META], e[:, :N_META],
                          pos[:N_META], pos[:N_META])
    n_blk = (T - N_META) // Q_BLOCK

    def block(i):
        start = N_META + i * Q_BLOCK
        qb = lax.dynamic_slice_in_dim(q, start, Q_BLOCK, axis=1)
        eb = lax.dynamic_slice_in_dim(e, start, Q_BLOCK, axis=1)
        return fox_attend(qb, k, v, eb, e, start + jnp.arange(Q_BLOCK), pos)

    out = lax.map(block, jnp.arange(n_blk))
    out = jnp.transpose(out, (1, 0, 2, 3, 4)).reshape(B, n_blk * Q_BLOCK, H, Dh)
    return jnp.concatenate([out_meta, out], axis=1)


def fox_sample(q, k, v, lf, k_past, v_past, lf_past):
    P = k_past.shape[1]
    Tn = q.shape[1]
    k_all = jnp.concatenate([k_past.astype(k.dtype), k], axis=1)
    v_all = jnp.concatenate([v_past.astype(v.dtype), v], axis=1)
    lf_all = jnp.concatenate([lf_past.astype(jnp.float32), lf], axis=1)
    e = lax.cumsum(lf_all, axis=1, reverse=True) - lf_all
    return fox_attend(q, k_all, v_all, e[:, P:], e, P + jnp.arange(Tn), jnp.arange(P + Tn))


def rwkv_time_mix(p, p_last, wkv0, lp):
    B, T, _ = p.shape
    prev = jnp.concatenate([p_last[:, None, :].astype(p.dtype), p[:, :-1]], axis=1)
    ps = p + (prev - p) * lp['rwkv_mu']
    r, k, v, wd, ad, gd = jnp.split(ps, SPLIT_RWKV, axis=-1)
    w = -jax.nn.softplus(-(lp['rwkv_w0'] + jnp.tanh(wd) @ lp['rwkv_w_up'])) - 0.5
    a = jax.nn.sigmoid(lp['rwkv_a0'] + ad @ lp['rwkv_a_up'])
    g = jax.nn.sigmoid(gd) @ lp['rwkv_g_up']

    def hs(t):
        return t.astype(jnp.float32).reshape(B, T, RWKV_HEADS, RWKV_HEAD_DIM)

    kk = hs(k * lp['rwkv_k_k'])
    kk = kk / jnp.maximum(jnp.linalg.norm(kk, axis=-1, keepdims=True), 1e-12)
    k = k * (1 + (a - 1) * lp['rwkv_k_a'])
    r_h, k_h, v_h, a_h = hs(r), hs(k), hs(v), hs(a)
    decay = jnp.exp(-jnp.exp(hs(w)))

    def step(S, inp):
        r_t, d_t, k_t, v_t, a_t, b_t = inp
        sa = jnp.einsum('bhij,bhj->bhi', S, a_t)
        S = S * d_t[:, :, None, :] + sa[..., None] * b_t[:, :, None, :] + v_t[..., None] * k_t[:, :, None, :]
        return S, jnp.einsum('bhij,bhj->bhi', S, r_t)

    seq = tuple(jnp.swapaxes(t, 0, 1) for t in (r_h, decay, k_h, v_h, -kk, kk * a_h))
    wkv, o = lax.scan(step, wkv0.astype(jnp.float32), seq)
    o = jnp.swapaxes(o, 0, 1)
    mean = jnp.mean(o, axis=-1, keepdims=True)
    var = jnp.mean(jnp.square(o - mean), axis=-1, keepdims=True)
    o = ((o - mean) * lax.rsqrt(var + GN_EPS)).reshape(B, T, RWKV_WIDTH)
    o = o * lp['rwkv_gn_g'].astype(jnp.float32) + lp['rwkv_gn_b'].astype(jnp.float32)
    bonus = jnp.sum(r_h * k_h * lp['rwkv_r_k'].astype(jnp.float32), axis=-1, keepdims=True) * v_h
    o = o + bonus.reshape(B, T, RWKV_WIDTH)
    return (o * g.astype(jnp.float32)).astype(p.dtype), wkv, p[:, -1]


def conv_ffn(h, conv_last, lp):
    u = h @ lp['ffn_w_gate']
    T = u.shape[1]
    ext = jnp.concatenate([conv_last.astype(u.dtype), u], axis=1)
    c = lp['ffn_conv_b'] + sum(lp['ffn_conv_w'][i] * ext[:, i:i + T] for i in range(CONV_WIDTH))
    y = (jax.nn.gelu(c, approximate=True) * (h @ lp['ffn_w_up'])) @ lp['ffn_w_down']
    return y, ext[:, T:]


def layer(x, attend, shift_last, wkv0, conv_last, lp):
    B, T, _ = x.shape
    h = rms_norm(x, lp['norm_mix_pre'])
    proj = h @ lp['w_in']
    fq, fk, fv, ff, pr, ga, gb = jnp.split(proj, SPLIT_IN, axis=-1)
    fq = fq.reshape(B, T, FOX_HEADS, FOX_HEAD_DIM)
    fk = fk.reshape(B, T, FOX_HEADS, FOX_HEAD_DIM)
    fv = fv.reshape(B, T, FOX_HEADS, FOX_HEAD_DIM)
    lf = jax.nn.log_sigmoid((ff + lp['fox_forget_bias']).astype(jnp.float32))
    fox_out = attend(fq, fk, fv, lf).reshape(B, T, FOX_WIDTH)
    rwkv_out, wkv_new, shift_new = rwkv_time_mix(pr, shift_last, wkv0, lp)
    mixed = jax.nn.sigmoid(ga) * (fox_out @ lp['w_o_fox']) + jax.nn.sigmoid(gb) * (rwkv_out @ lp['w_o_rwkv'])
    x = x + rms_norm(mixed @ lp['w_out'], lp['norm_mix_post'])
    f, conv_new = conv_ffn(rms_norm(x, lp['norm_ffn_pre']), conv_last, lp)
    x = x + rms_norm(f, lp['norm_ffn_post'])
    return x, (fk, fv, lf, wkv_new, shift_new, conv_new)


def setup_inputs(seed: int = 0) -> dict:
    key = jax.random.key(seed)
    ks = jax.random.split(key, 48)
    counter = [0]

    def nxt():
        counter[0] += 1
        return ks[counter[0] - 1]

    def nrm(shape, scale):
        return scale * jax.random.normal(nxt(), shape, jnp.float32)

    n_pages = PAST_LEN // PAGE_SIZE
    n_used = DEC_BATCH * n_pages
    n_pool = n_used + n_used // 4
    page_table = jax.random.permutation(nxt(), n_pool)[:n_used].reshape(DEC_BATCH, n_pages).astype(jnp.int32)
    return {
        'x_prompt': nrm((BATCH, SEQ, D_MODEL), 1.0),
        'x_sample': nrm((DEC_BATCH, DEC_SEQ, D_MODEL), 1.0),
        'cache_k': nrm((n_pool, DEPTH, PAGE_SIZE, FOX_HEADS, FOX_HEAD_DIM), 1.0),
        'cache_v': nrm((n_pool, DEPTH, PAGE_SIZE, FOX_HEADS, FOX_HEAD_DIM), 1.0),
        'cache_logf': jax.nn.log_sigmoid(8.0 + nrm((n_pool, DEPTH, PAGE_SIZE, FOX_HEADS), 1.0)),
        'state_wkv': nrm((DEC_BATCH, DEPTH, RWKV_HEADS, RWKV_HEAD_DIM, RWKV_HEAD_DIM), 0.1),
        'state_shift': nrm((DEC_BATCH, DEPTH, RWKV_PROJ), 1.0),
        'state_conv': nrm((DEC_BATCH, DEPTH, CONV_WIDTH - 1, D_FF), 1.0),
        'page_table': page_table,
        'meta_tokens': nrm((N_META, D_MODEL), 1.0),
        'norm_mix_pre': 1.0 + nrm((DEPTH, D_MODEL), 0.05),
        'norm_mix_post': 1.0 + nrm((DEPTH, D_MODEL), 0.05),
        'norm_ffn_pre': 1.0 + nrm((DEPTH, D_MODEL), 0.05),
        'norm_ffn_post': 1.0 + nrm((DEPTH, D_MODEL), 0.05),
        'w_in': nrm((DEPTH, D_MODEL, IN_PROJ), D_MODEL ** -0.5),
        'fox_forget_bias': 3.0 + 5.0 * jax.random.uniform(nxt(), (DEPTH, FOX_HEADS), jnp.float32),
        'rwkv_mu': jax.random.uniform(nxt(), (DEPTH, RWKV_PROJ), jnp.float32),
        'rwkv_w0': -1.5 + nrm((DEPTH, RWKV_WIDTH), 0.5),
        'rwkv_w_up': nrm((DEPTH, DECAY_LORA, RWKV_WIDTH), 0.5 * DECAY_LORA ** -0.5),
        'rwkv_a0': nrm((DEPTH, RWKV_WIDTH), 0.1),
        'rwkv_a_up': nrm((DEPTH, AAA_LORA, RWKV_WIDTH), 0.5 * AAA_LORA ** -0.5),
        'rwkv_g_up': nrm((DEPTH, GATE_LORA, RWKV_WIDTH), GATE_LORA ** -0.5),
        'rwkv_k_k': 0.85 + nrm((DEPTH, RWKV_WIDTH), 0.05),
        'rwkv_k_a': 1.0 + nrm((DEPTH, RWKV_WIDTH), 0.05),
        'rwkv_r_k': nrm((DEPTH, RWKV_HEADS, RWKV_HEAD_DIM), 0.1),
        'rwkv_gn_g': 1.0 + nrm((DEPTH, RWKV_WIDTH), 0.05),
        'rwkv_gn_b': nrm((DEPTH, RWKV_WIDTH), 0.02),
        'w_o_fox': nrm((DEPTH, FOX_WIDTH, D_MODEL), FOX_WIDTH ** -0.5),
        'w_o_rwkv': nrm((DEPTH, RWKV_WIDTH, D_MODEL), RWKV_WIDTH ** -0.5),
        'w_out': nrm((DEPTH, D_MODEL, D_MODEL), D_MODEL ** -0.5),
        'ffn_w_gate': nrm((DEPTH, D_MODEL, D_FF), D_MODEL ** -0.5),
        'ffn_w_up': nrm((DEPTH, D_MODEL, D_FF), D_MODEL ** -0.5),
        'ffn_conv_w': nrm((DEPTH, CONV_WIDTH, D_FF), CONV_WIDTH ** -0.5),
        'ffn_conv_b': nrm((DEPTH, D_FF), 0.02),
        'ffn_w_down': nrm((DEPTH, D_FF, D_MODEL), D_FF ** -0.5),
    }


def reference(x_prompt, x_sample, cache_k, cache_v, cache_logf, state_wkv, state_shift, state_conv, page_table,
              meta_tokens, norm_mix_pre, norm_mix_post, norm_ffn_pre, norm_ffn_post, w_in, fox_forget_bias,
              rwkv_mu, rwkv_w0, rwkv_w_up, rwkv_a0, rwkv_a_up, rwkv_g_up, rwkv_k_k, rwkv_k_a, rwkv_r_k,
              rwkv_gn_g, rwkv_gn_b, w_o_fox, w_o_rwkv, w_out, ffn_w_gate, ffn_w_up, ffn_conv_w, ffn_conv_b,
              ffn_w_down):
    B = x_prompt.shape[0]
    DB = x_sample.shape[0]
    meta = jnp.broadcast_to(meta_tokens[None].astype(x_prompt.dtype), (B, N_META, D_MODEL))
    xp = jnp.concatenate([meta, x_prompt], axis=1)
    xs = x_sample
    p_rows, s_rows = [], []
    for l in range(DEPTH):
        lp = {
            'norm_mix_pre': norm_mix_pre[l], 'norm_mix_post': norm_mix_post[l],
            'norm_ffn_pre': norm_ffn_pre[l], 'norm_ffn_post': norm_ffn_post[l],
            'w_in': w_in[l], 'fox_forget_bias': fox_forget_bias[l],
            'rwkv_mu': rwkv_mu[l], 'rwkv_w0': rwkv_w0[l], 'rwkv_w_up': rwkv_w_up[l], 'rwkv_a0': rwkv_a0[l],
            'rwkv_a_up': rwkv_a_up[l], 'rwkv_g_up': rwkv_g_up[l], 'rwkv_k_k': rwkv_k_k[l], 'rwkv_k_a': rwkv_k_a[l],
            'rwkv_r_k': rwkv_r_k[l], 'rwkv_gn_g': rwkv_gn_g[l], 'rwkv_gn_b': rwkv_gn_b[l],
            'w_o_fox': w_o_fox[l], 'w_o_rwkv': w_o_rwkv[l], 'w_out': w_out[l],
            'ffn_w_gate': ffn_w_gate[l], 'ffn_w_up': ffn_w_up[l], 'ffn_conv_w': ffn_conv_w[l],
            'ffn_conv_b': ffn_conv_b[l], 'ffn_w_down': ffn_w_down[l],
        }
        xp, st_p = layer(xp, fox_prompt, jnp.zeros((B, RWKV_PROJ), xp.dtype),
                         jnp.zeros((B, RWKV_HEADS, RWKV_HEAD_DIM, RWKV_HEAD_DIM), jnp.float32),
                         jnp.zeros((B, CONV_WIDTH - 1, D_FF), xp.dtype), lp)
        k_past = cache_k[page_table, l].reshape(DB, -1, FOX_HEADS, FOX_HEAD_DIM)
        v_past = cache_v[page_table, l].reshape(DB, -1, FOX_HEADS, FOX_HEAD_DIM)
        lf_past = cache_logf[page_table, l].reshape(DB, -1, FOX_HEADS)
        attend_s = functools.partial(fox_sample, k_past=k_past, v_past=v_past, lf_past=lf_past)
        xs, st_s = layer(xs, attend_s, state_shift[:, l], state_wkv[:, l], state_conv[:, l], lp)
        p_rows.append(st_p)
        s_rows.append(st_s)

    def stack(rows, i):
        return jnp.stack([r[i] for r in rows], axis=1)

    y_prompt = xp[:, N_META:]
    return (y_prompt, xs,
            stack(p_rows, 0), stack(p_rows, 1), stack(p_rows, 2), stack(p_rows, 3), stack(p_rows, 4), stack(p_rows, 5),
            stack(s_rows, 0), stack(s_rows, 1), stack(s_rows, 2), stack(s_rows, 3), stack(s_rows, 4), stack(s_rows, 5))
```

```python
import functools

import jax
import jax.numpy as jnp
from jax import lax
from jax.experimental import pallas as pl
from jax.experimental.pallas import tpu as pltpu

F32 = jnp.float32
BF16 = jnp.bfloat16
HIGHEST = lax.Precision.HIGHEST

NORM_EPS = 1e-6
GN_EPS = 64e-5
NEG_INF = -1e30

LANES = 128
SUBLANES = 8
CHUNK = 64
MIB = 2 ** 20


def _round_up(x, m):
    return (x + m - 1) // m * m


def _pick_tile(n, target, align):
    best = None
    for d in range(align, min(n, target) + 1, align):
        if n % d == 0:
            best = d
    return n if best is None else best


def _params(sem, vmem_mib):
    return pltpu.CompilerParams(dimension_semantics=sem, vmem_limit_bytes=int(vmem_mib * MIB))


def _rms(x, g):
    return x * lax.rsqrt(jnp.mean(x * x, axis=-1, keepdims=True) + NORM_EPS) * g


def _rmsnorm_kernel(x_ref, g_ref, o_ref):
    o_ref[...] = _rms(x_ref[...], g_ref[...]).astype(o_ref.dtype)


def rmsnorm_bf16(x, g):
    M, D = x.shape
    tr = _pick_tile(M, 512, 16)
    return pl.pallas_call(
        _rmsnorm_kernel,
        grid=(M // tr,),
        in_specs=[pl.BlockSpec((tr, D), lambda i: (i, 0)), pl.BlockSpec((1, D), lambda i: (0, 0))],
        out_specs=pl.BlockSpec((tr, D), lambda i: (i, 0)),
        out_shape=jax.ShapeDtypeStruct((M, D), BF16),
        compiler_params=_params(("parallel",), 48),
        name="rmsnorm",
    )(x, g.reshape(1, D))


def _resnorm_kernel(y_ref, x_ref, gp_ref, gn_ref, xo_ref, ho_ref):
    xn = x_ref[...] + _rms(y_ref[...], gp_ref[...])
    xo_ref[...] = xn
    ho_ref[...] = _rms(xn, gn_ref[...]).astype(ho_ref.dtype)


def _resnorm_last_kernel(y_ref, x_ref, gp_ref, xo_ref):
    xo_ref[...] = x_ref[...] + _rms(y_ref[...], gp_ref[...])


def resnorm(y, x, g_post, g_next):
    M, D = x.shape
    tr = _pick_tile(M, 512, 16)
    row = pl.BlockSpec((tr, D), lambda i: (i, 0))
    vec = pl.BlockSpec((1, D), lambda i: (0, 0))
    if g_next is None:
        return pl.pallas_call(
            _resnorm_last_kernel, grid=(M // tr,), in_specs=[row, row, vec], out_specs=row,
            out_shape=jax.ShapeDtypeStruct((M, D), F32),
            compiler_params=_params(("parallel",), 48), name="resnorm_last",
        )(y, x, g_post.reshape(1, D)), None
    return pl.pallas_call(
        _resnorm_kernel, grid=(M // tr,), in_specs=[row, row, vec, vec], out_specs=[row, row],
        out_shape=[jax.ShapeDtypeStruct((M, D), F32), jax.ShapeDtypeStruct((M, D), BF16)],
        compiler_params=_params(("parallel",), 56), name="resnorm",
    )(y, x, g_post.reshape(1, D), g_next.reshape(1, D))


def _mm_kernel(a_ref, w_ref, o_ref):
    o_ref[...] = jnp.dot(a_ref[...], w_ref[...], preferred_element_type=F32).astype(o_ref.dtype)


def _mm_acc_kernel(a_ref, w_ref, o_ref, acc_ref):
    k = pl.program_id(2)

    @pl.when(k == 0)
    def _():
        acc_ref[...] = jnp.zeros_like(acc_ref)

    acc_ref[...] += jnp.dot(a_ref[...], w_ref[...], preferred_element_type=F32)

    @pl.when(k == pl.num_programs(2) - 1)
    def _():
        o_ref[...] = acc_ref[...].astype(o_ref.dtype)


def matmul(a, w, out_dtype=F32, tm_target=1024, tn_target=512, tk_target=6144):
    M, K = a.shape
    N = w.shape[1]
    tm = _pick_tile(M, tm_target, 16)
    tn = _pick_tile(N, tn_target, LANES)
    tk = _pick_tile(K, tk_target, LANES)
    nk = K // tk
    osz = jnp.dtype(out_dtype).itemsize
    vmem = 2 * (tm * tk * 2 + tk * tn * 2 + tm * tn * osz) + tm * tn * 4
    vmem_mib = vmem / MIB + 8
    if nk == 1:
        return pl.pallas_call(
            _mm_kernel,
            grid=(M // tm, N // tn),
            in_specs=[pl.BlockSpec((tm, K), lambda i, j: (i, 0)), pl.BlockSpec((K, tn), lambda i, j: (0, j))],
            out_specs=pl.BlockSpec((tm, tn), lambda i, j: (i, j)),
            out_shape=jax.ShapeDtypeStruct((M, N), out_dtype),
            compiler_params=_params(("parallel", "parallel"), vmem_mib),
            name="matmul",
        )(a, w)
    return pl.pallas_call(
        _mm_acc_kernel,
        grid=(M // tm, N // tn, nk),
        in_specs=[pl.BlockSpec((tm, tk), lambda i, j, k: (i, k)), pl.BlockSpec((tk, tn), lambda i, j, k: (k, j))],
        out_specs=pl.BlockSpec((tm, tn), lambda i, j, k: (i, j)),
        out_shape=jax.ShapeDtypeStruct((M, N), out_dtype),
        scratch_shapes=[pltpu.VMEM((tm, tn), F32)],
        compiler_params=_params(("parallel", "parallel", "arbitrary"), vmem_mib),
        name="matmul_acc",
    )(a, w)


def _merge_kernel(a1_ref, w1_ref, a2_ref, w2_ref, ga_ref, gb_ref, o_ref):
    f = jnp.dot(a1_ref[...], w1_ref[...], preferred_element_type=F32)
    r = jnp.dot(a2_ref[...], w2_ref[...], preferred_element_type=F32)
    o_ref[...] = (jax.nn.sigmoid(ga_ref[...]) * f + jax.nn.sigmoid(gb_ref[...]) * r).astype(o_ref.dtype)


def gated_merge(fox, w_of, rwkv, w_or, proj, ga_off, gb_off):
    M, K1 = fox.shape
    K2 = rwkv.shape[1]
    D = w_of.shape[1]
    tm = _pick_tile(M, 1024, 16)
    tn = _pick_tile(D, 512, LANES)
    assert ga_off % tn == 0 and gb_off % tn == 0
    ja, jb = ga_off // tn, gb_off // tn
    vmem = 2 * (tm * (K1 + K2) * 2 + (K1 + K2) * tn * 2 + 2 * tm * tn * 4 + tm * tn * 2) + 2 * tm * tn * 4
    return pl.pallas_call(
        _merge_kernel,
        grid=(M // tm, D // tn),
        in_specs=[
            pl.BlockSpec((tm, K1), lambda i, j: (i, 0)), pl.BlockSpec((K1, tn), lambda i, j: (0, j)),
            pl.BlockSpec((tm, K2), lambda i, j: (i, 0)), pl.BlockSpec((K2, tn), lambda i, j: (0, j)),
            pl.BlockSpec((tm, tn), lambda i, j: (i, ja + j)), pl.BlockSpec((tm, tn), lambda i, j: (i, jb + j)),
        ],
        out_specs=pl.BlockSpec((tm, tn), lambda i, j: (i, j)),
        out_shape=jax.ShapeDtypeStruct((M, D), BF16),
        compiler_params=_params(("parallel", "parallel"), vmem / MIB + 8),
        name="gated_merge",
    )(fox, w_of, rwkv, w_or, proj, proj)


def _shift_rows(x, fill_rows):
    n = len(fill_rows)
    out = pltpu.roll(x, n, axis=0)
    row = lax.broadcasted_iota(jnp.int32, x.shape, 0)
    for i, f in enumerate(fill_rows):
        out = jnp.where(row == i, f, out)
    return out


def _rwkv_prep_kernel(r_ref, k_ref, v_ref, t_ref, rh_ref, kh_ref, vh_ref, th_ref,
                      r0_ref, k0_ref, v0_ref, t0_ref, mur_ref, muk_ref, muv_ref, mut_ref,
                      w0_ref, a0_ref, kk_ref, ka_ref, wup_ref, aup_ref, gup_ref,
                      ro_ref, ld_ref, ko_ref, vo_ref, ao_ref, bo_ref, go_ref):
    first = pl.program_id(1) == 0

    def tokshift(x_ref, h_ref, s_ref, mu_ref):
        x = x_ref[...]
        prev0 = jnp.where(first, s_ref[...], h_ref[SUBLANES - 1:SUBLANES, :])
        prev = _shift_rows(x, [prev0])
        return x + (prev - x) * mu_ref[...]

    r = tokshift(r_ref, rh_ref, r0_ref, mur_ref)
    k = tokshift(k_ref, kh_ref, k0_ref, muk_ref)
    v = tokshift(v_ref, vh_ref, v0_ref, muv_ref)
    t = tokshift(t_ref, th_ref, t0_ref, mut_ref)

    wlin = w0_ref[...] + jnp.dot(jnp.tanh(t).astype(BF16), wup_ref[...], preferred_element_type=F32)
    w = -jax.nn.softplus(-wlin) - 0.5
    a = jax.nn.sigmoid(a0_ref[...] + jnp.dot(t.astype(BF16), aup_ref[...], preferred_element_type=F32))
    g = jnp.dot(jax.nn.sigmoid(t).astype(BF16), gup_ref[...], preferred_element_type=F32)

    kkp = k * kk_ref[...]
    sq = kkp * kkp
    cw = sq.shape[1]
    row = lax.broadcasted_iota(jnp.int32, (LANES, LANES), 0)
    col = lax.broadcasted_iota(jnp.int32, (LANES, LANES), 1)
    ones_bd = jnp.where(row // CHUNK == col // CHUNK, 1.0, 0.0).astype(F32)
    parts = [jnp.dot(sq[:, c:c + LANES], ones_bd, precision=HIGHEST, preferred_element_type=F32)
             for c in range(0, cw, LANES)]
    nrm = jnp.sqrt(jnp.concatenate(parts, axis=1) if len(parts) > 1 else parts[0])
    kk = kkp / jnp.maximum(nrm, 1e-12)

    ro_ref[...] = r
    ld_ref[...] = -jnp.exp(w)
    ko_ref[...] = k * (1.0 + (a - 1.0) * ka_ref[...])
    vo_ref[...] = v
    ao_ref[...] = -kk
    bo_ref[...] = kk * a
    go_ref[...] = g


def rwkv_prep(proj, lay, grp, shift0, mu, w0, a0, k_k, k_a, wup, aup, gup):
    RW, TW = lay["RW"], lay["TW"]
    n_seq, T, row0 = grp
    R = _pick_tile(T, 512, SUBLANES)
    cb = _pick_tile(RW, 512, LANES)
    nR, nC = T // R, RW // cb
    rb0 = row0 // R
    hb0 = row0 // SUBLANES
    assert row0 % R == 0 and lay["r_off"] % cb == 0 and lay["t_off"] % TW == 0
    cr, ck, cv = (lay[n] // cb for n in ("r_off", "rk_off", "rv_off"))
    ct = lay["t_off"] // TW

    def main(c0):
        return pl.BlockSpec((R, cb), lambda s, i, c: (rb0 + s * nR + i, c0 + c))

    def halo(c0):
        return pl.BlockSpec((SUBLANES, cb),
                            lambda s, i, c: (jnp.maximum(hb0 + (s * T + i * R) // SUBLANES - 1, 0), c0 + c))

    tail = pl.BlockSpec((R, TW), lambda s, i, c: (rb0 + s * nR + i, ct))
    tail_h = pl.BlockSpec((SUBLANES, TW),
                          lambda s, i, c: (jnp.maximum(hb0 + (s * T + i * R) // SUBLANES - 1, 0), ct))
    s0 = pl.BlockSpec((None, 1, cb), lambda s, i, c: (s, 0, c))
    s0t = pl.BlockSpec((None, 1, TW), lambda s, i, c: (s, 0, 0))
    vec = pl.BlockSpec((1, cb), lambda s, i, c: (0, c))
    vect = pl.BlockSpec((1, TW), lambda s, i, c: (0, 0))
    lora = pl.BlockSpec((TW, cb), lambda s, i, c: (0, c))
    out = pl.BlockSpec((R, cb), lambda s, i, c: (s * nR + i, c))
    in_specs = [main(cr), main(ck), main(cv), tail, halo(cr), halo(ck), halo(cv), tail_h,
                s0, s0, s0, s0t, vec, vec, vec, vect, vec, vec, vec, vec, lora, lora, lora]
    args = [proj, proj, proj, proj, proj, proj, proj, proj, *shift0, *mu, w0, a0, k_k, k_a, wup, aup, gup]
    return pl.pallas_call(
        _rwkv_prep_kernel,
        grid=(n_seq, nR, nC),
        in_specs=in_specs,
        out_specs=[out] * 7,
        out_shape=[jax.ShapeDtypeStruct((n_seq * T, RW), F32)] * 7,
        compiler_params=_params(("parallel", "parallel", "parallel"), 48),
        name="rwkv_prep",
    )(*args)


def _seg_sum(x, lo):
    s0 = jnp.sum(jnp.where(lo, x, 0.0), axis=1, keepdims=True)
    s1 = jnp.sum(jnp.where(lo, 0.0, x), axis=1, keepdims=True)
    return jnp.where(lo, s0, s1)


def _rwkv_scan_kernel(r_ref, ld_ref, k_ref, v_ref, a_ref, b_ref, g_ref, s0_ref, rk_ref, gng_ref, gnb_ref,
                      o_ref, so_ref, st_ref, *, n_chunks, n0, rows, n_dbl, pairs):
    C = CHUNK
    row = lax.broadcasted_iota(jnp.int32, (C, LANES), 0)
    lane = lax.broadcasted_iota(jnp.int32, (C, LANES), 1)
    lane_in = lane % C
    lo = lane < C
    m_strict = lane_in < row
    m_incl = lane_in <= row
    eye_pair = jnp.where(lane_in == row, 1.0, 0.0).astype(F32)
    row2 = lax.broadcasted_iota(jnp.int32, (2 * C, LANES), 0)
    lane2 = lax.broadcasted_iota(jnp.int32, (2 * C, LANES), 1)
    bdm = (row2 // C) == (lane2 // C)
    eye2 = row2 == lane2
    tr = lax.broadcasted_iota(jnp.int32, (C, C), 0)
    tc = lax.broadcasted_iota(jnp.int32, (C, C), 1)
    tril = jnp.where(tc <= tr, 1.0, 0.0).astype(F32)

    def bd(x):
        return jnp.where(bdm, jnp.concatenate([x, x], axis=0), 0.0)

    def mm(x, y):
        return jnp.dot(x.astype(BF16), y.astype(BF16), preferred_element_type=F32)

    def mm_tn(x, y):
        return mm(x.T, y)

    def mm_nt(x, y):
        return lax.dot_general(x.astype(BF16), y.astype(BF16), (((1,), (1,)), ((), ())),
                               preferred_element_type=F32)

    for p in range(pairs):
        st_ref[p] = jnp.where(bdm, jnp.concatenate([s0_ref[p], s0_ref[p]], axis=1), 0.0)

    def chunk(c, carry):
        if n_chunks == 1:
            start = 0
            nact = n0
        else:
            start = pl.multiple_of(jnp.where(c == 0, 0, n0 + (c - 1) * C), 16)
            nact = jnp.where(c == 0, n0, C)
        act = row < nact

        for p in range(pairs):
            ls = slice(p * LANES, (p + 1) * LANES)

            def load(ref):
                if rows >= C:
                    return ref[pl.ds(start, C), ls]
                x = ref[:, ls]
                return jnp.concatenate([x, jnp.zeros((C - rows, LANES), x.dtype)], axis=0)

            r_raw, k_raw, v_raw = load(r_ref), load(k_ref), load(v_ref)
            logd = jnp.where(act, load(ld_ref), 0.0)
            r = jnp.where(act, r_raw, 0.0)
            k = jnp.where(act, k_raw, 0.0)
            v = jnp.where(act, v_raw, 0.0)
            a = jnp.where(act, load(a_ref), 0.0)
            b = jnp.where(act, load(b_ref), 0.0)

            lw = jnp.dot(tril, logd, precision=HIGHEST, preferred_element_type=F32)
            lw_end = lw[C - 1:C, :]
            dec_in = jnp.exp(lw)
            dec_out = jnp.exp(-lw)
            at = a * jnp.exp(lw - logd)
            rt = r * dec_in
            bt = b * dec_out
            kt = k * dec_out
            dec_tail = jnp.exp(lw_end - lw)
            btl = b * dec_tail
            ktl = k * dec_tail

            big = mm_nt(jnp.concatenate([at, rt], axis=0), jnp.concatenate([bd(bt), bd(kt)], axis=0))
            a_ab = jnp.where(m_strict, big[0:C, 0:LANES], 0.0)
            a_ak = jnp.where(m_strict, big[0:C, LANES:2 * LANES], 0.0)
            a_rb = jnp.where(m_incl, big[C:2 * C, 0:LANES], 0.0)
            a_rk = jnp.where(m_incl, big[C:2 * C, LANES:2 * LANES], 0.0)

            pw = a_ab
            x = eye_pair + a_ab
            for _ in range(n_dbl):
                pw = mm(pw, bd(pw))
                x = x + mm(x, bd(pw))

            av_op = mm(jnp.concatenate([a_ak, a_rk], axis=0), bd(v))
            av, o_pre = av_op[0:C], av_op[C:2 * C]
            z = mm(x, jnp.concatenate([bd(at), bd(av)], axis=1))
            x1, u_pre = z[:, 0:LANES], z[:, LANES:2 * LANES]
            y = mm(a_rb, jnp.concatenate([bd(x1), bd(u_pre)], axis=1))
            gmat = rt + y[:, 0:LANES]
            hmat = o_pre + y[:, LANES:2 * LANES]
            m_mat = jnp.where(eye2, jnp.broadcast_to(jnp.exp(lw_end), (2 * C, LANES)), 0.0) + jnp.where(
                bdm, mm_tn(btl, x1), 0.0)
            n_mat = jnp.where(bdm, mm_tn(jnp.concatenate([btl, ktl], axis=0),
                                         jnp.concatenate([u_pre, v], axis=0)), 0.0)

            st = st_ref[p]
            o = mm(gmat, st) + hmat
            st_ref[p] = mm(m_mat, st) + n_mat

            mean = _seg_sum(o, lo) * (1.0 / C)
            oc = o - mean
            var = _seg_sum(oc * oc, lo) * (1.0 / C)
            on = oc * lax.rsqrt(var + GN_EPS) * gng_ref[:, ls] + gnb_ref[:, ls]
            bonus = _seg_sum(r_raw * k_raw * rk_ref[:, ls], lo) * v_raw
            g = load(g_ref)
            res = ((on + bonus) * g).astype(o_ref.dtype)
            if rows >= C:
                o_ref[pl.ds(start, C), ls] = res
            else:
                o_ref[:, ls] = res[0:rows]
        return carry

    if n_chunks == 1:
        chunk(0, 0)
    else:
        lax.fori_loop(0, n_chunks, chunk, 0)
    for p in range(pairs):
        so_ref[p] = st_ref[p]


def rwkv_scan(prep, n_seq, T, s0t, r_k, gn_g, gn_b, n0, out_rows):
    RW = prep[0].shape[1]
    pairs = 2 if (RW // LANES) % 2 == 0 else 1
    cb = pairs * LANES
    nP = RW // cb
    if T >= CHUNK:
        assert (T - n0) % CHUNK == 0 and n0 % 16 == 0
        n_chunks = 1 + (T - n0) // CHUNK
        nmax = CHUNK
    else:
        assert T == n0
        n_chunks = 1
        nmax = n0
    n_dbl = max(0, (nmax - 1).bit_length() - 1)
    seq = pl.BlockSpec((T, cb), lambda s, p: (s, p))
    vec = pl.BlockSpec((1, cb), lambda s, p: (0, p))
    st_in = pl.BlockSpec((None, pairs, LANES, CHUNK), lambda s, p: (s, p, 0, 0))
    st_out = pl.BlockSpec((None, pairs, LANES, LANES), lambda s, p: (s, p, 0, 0))
    out_dtype = BF16 if T % 16 == 0 else F32
    return pl.pallas_call(
        functools.partial(_rwkv_scan_kernel, n_chunks=n_chunks, n0=n0, rows=T, n_dbl=n_dbl, pairs=pairs),
        grid=(n_seq, nP),
        in_specs=[seq] * 7 + [st_in, vec, vec, vec],
        out_specs=[seq, st_out],
        out_shape=[jax.ShapeDtypeStruct((out_rows, RW), out_dtype),
                   jax.ShapeDtypeStruct((n_seq, RW // LANES, LANES, LANES), F32)],
        scratch_shapes=[pltpu.VMEM((pairs, LANES, LANES), F32)],
        compiler_params=_params(("parallel", "parallel"), 48),
        name="rwkv_scan",
    )(*prep, s0t, r_k, gn_g, gn_b)


def _fox_lf_kernel(ff_ref, bias_ref, lf_ref, c_ref, *, tb):
    T = ff_ref.shape[0]
    z = ff_ref[...] + bias_ref[...]
    lf = jnp.minimum(z, 0.0) - jnp.log1p(jnp.exp(-jnp.abs(z)))
    lf_ref[...] = lf
    for r0 in range(0, T, tb):
        row = lax.broadcasted_iota(jnp.int32, (tb, T), 0) + r0
        col = lax.broadcasted_iota(jnp.int32, (tb, T), 1)
        tri = jnp.where(col <= row, 1.0, 0.0).astype(F32)
        c_ref[r0:r0 + tb, :] = jnp.dot(tri, lf, precision=HIGHEST, preferred_element_type=F32)


def fox_lf(ff, bias):
    n_seq, T, H = ff.shape
    tb = _pick_tile(T, 512, SUBLANES)
    blk = pl.BlockSpec((None, T, H), lambda s: (s, 0, 0))
    return pl.pallas_call(
        functools.partial(_fox_lf_kernel, tb=tb),
        grid=(n_seq,),
        in_specs=[blk, pl.BlockSpec((1, H), lambda s: (0, 0))],
        out_specs=[blk, blk],
        out_shape=[jax.ShapeDtypeStruct((n_seq, T, H), F32)] * 2,
        compiler_params=_params(("parallel",), 48),
        name="fox_lf",
    )(ff, bias.reshape(1, H))


def _fox_prompt_kernel(q_ref, k_ref, v_ref, c_ref, ct_ref, o_ref, *, tq, scale):
    T, Dh = q_ref.shape
    nb = T // tq
    H = c_ref.shape[1]
    h = pl.program_id(1)
    hl = lax.broadcasted_iota(jnp.int32, (T, H), 1)
    ccol = jnp.sum(jnp.where(hl == h, c_ref[...], 0.0), axis=1, keepdims=True)
    ri = lax.broadcasted_iota(jnp.int32, (tq, tq), 0)
    ci = lax.broadcasted_iota(jnp.int32, (tq, tq), 1)

    for qi in range(nb):
        q = q_ref[qi * tq:(qi + 1) * tq, :].astype(BF16)
        cq = ccol[qi * tq:(qi + 1) * tq, :]

        def kv_step(kj, carry):
            m, l, acc = carry
            k0 = pl.multiple_of(kj * tq, SUBLANES)
            kb = k_ref[pl.ds(k0, tq), :].astype(BF16)
            vb = v_ref[pl.ds(k0, tq), :].astype(BF16)
            s = lax.dot_general(q, kb, (((1,), (1,)), ((), ())), preferred_element_type=F32) * scale
            s = s + (cq - ct_ref[h, pl.ds(kj, 1), :])
            s = jnp.where(ci + (kj - qi) * tq <= ri, s, NEG_INF)
            m_new = jnp.maximum(m, jnp.max(s, axis=1, keepdims=True))
            alpha = jnp.exp(m - m_new)
            p = jnp.exp(s - m_new)
            l = alpha * l + jnp.sum(p, axis=1, keepdims=True)
            acc = alpha * acc + jnp.dot(p.astype(BF16), vb, preferred_element_type=F32)
            return m_new, l, acc

        init = (jnp.full((tq, 1), NEG_INF, F32), jnp.zeros((tq, 1), F32), jnp.zeros((tq, Dh), F32))
        m, l, acc = lax.fori_loop(0, qi + 1, kv_step, init)
        o_ref[qi * tq:(qi + 1) * tq, :] = (acc / l).astype(o_ref.dtype)


def fox_prompt(proj, lay, B, T, c, out_rows):
    FH, Dh = lay["FH"], lay["Dh"]
    tq = _pick_tile(T, 512, SUBLANES)
    nb = T // tq
    ct = jnp.swapaxes(c, 1, 2).reshape(B, FH, nb, tq)
    assert Dh % LANES == 0 and lay["q_off"] % Dh == 0
    cq, ck, cv = (lay[n] // Dh for n in ("q_off", "k_off", "v_off"))
    return pl.pallas_call(
        functools.partial(_fox_prompt_kernel, tq=tq, scale=Dh ** -0.5),
        grid=(B, FH),
        in_specs=[
            pl.BlockSpec((T, Dh), lambda b, h: (b, cq + h)),
            pl.BlockSpec((T, Dh), lambda b, h: (b, ck + h)),
            pl.BlockSpec((T, Dh), lambda b, h: (b, cv + h)),
            pl.BlockSpec((None, T, FH), lambda b, h: (b, 0, 0)),
            pl.BlockSpec((None, FH, nb, tq), lambda b, h: (b, 0, 0, 0)),
        ],
        out_specs=pl.BlockSpec((T, Dh), lambda b, h: (b, h)),
        out_shape=jax.ShapeDtypeStruct((out_rows, FH * Dh), BF16),
        compiler_params=_params(("parallel", "parallel"), 48),
        name="fox_prompt",
    )(proj, proj, proj, c, ct)


def _fox_decode_kernel(pt_ref, q_ref, kn_ref, vn_ref, kp_ref, vp_ref, lfp_ref, ccol_ref, crow_ref,
                       o_ref, m_ref, l_ref, acc_ref, r_ref, *, FH, Dh, scale):
    j = pl.program_id(1)
    nj = pl.num_programs(1)
    P = kp_ref.shape[0]
    TS = q_ref.shape[0]

    @pl.when(j == 0)
    def _():
        m_ref[...] = jnp.full_like(m_ref, NEG_INF)
        l_ref[...] = jnp.zeros_like(l_ref)
        acc_ref[...] = jnp.zeros_like(acc_ref)
        r_ref[...] = jnp.zeros_like(r_ref)

    lfp = lfp_ref[...]
    rr = lax.broadcasted_iota(jnp.int32, (P, P), 0)
    cc = lax.broadcasted_iota(jnp.int32, (P, P), 1)
    upper = jnp.where(rr > cc, 1.0, 0.0).astype(F32)
    wsum = jnp.dot(lfp, upper, precision=HIGHEST, preferred_element_type=F32) + r_ref[...]

    def update(h, s, vb):
        m_old = m_ref[h]
        m_new = jnp.maximum(m_old, jnp.max(s, axis=1, keepdims=True))
        alpha = jnp.exp(m_old - m_new)
        p = jnp.exp(s - m_new)
        l_ref[h] = alpha * l_ref[h] + jnp.sum(p, axis=1, keepdims=True)
        hs = slice(h * Dh, (h + 1) * Dh)
        acc_ref[:, hs] = alpha * acc_ref[:, hs] + jnp.dot(p.astype(BF16), vb, preferred_element_type=F32)
        m_ref[h] = m_new

    for h in range(FH):
        hs = slice(h * Dh, (h + 1) * Dh)
        q = q_ref[:, hs].astype(BF16)
        s = lax.dot_general(q, kp_ref[:, hs].astype(BF16), (((1,), (1,)), ((), ())),
                            preferred_element_type=F32) * scale
        s = s + wsum[h:h + 1, :] + ccol_ref[h]
        update(h, s, vp_ref[:, hs].astype(BF16))

    r_ref[...] = r_ref[...] + jnp.sum(lfp, axis=1, keepdims=True)

    @pl.when(j == nj - 1)
    def _():
        ri = lax.broadcasted_iota(jnp.int32, (TS, TS), 0)
        ci = lax.broadcasted_iota(jnp.int32, (TS, TS), 1)
        for h in range(FH):
            hs = slice(h * Dh, (h + 1) * Dh)
            q = q_ref[:, hs].astype(BF16)
            s = lax.dot_general(q, kn_ref[:, hs].astype(BF16), (((1,), (1,)), ((), ())),
                                preferred_element_type=F32) * scale
            s = jnp.where(ci <= ri, s + (ccol_ref[h] - crow_ref[h]), NEG_INF)
            update(h, s, vn_ref[:, hs].astype(BF16))
            o_ref[:, hs] = (acc_ref[:, hs] / l_ref[h]).astype(o_ref.dtype)


def fox_decode(proj, lay, cache_k4, cache_v4, cache_lft, page_table, layer, c_new, row0):
    FH, Dh = lay["FH"], lay["Dh"]
    FW = FH * Dh
    DB, NP = page_table.shape
    PAGE = cache_k4.shape[2]
    TS = c_new.shape[1]
    assert TS == SUBLANES and row0 % TS == 0 and lay["q_off"] % FW == 0
    rb0 = row0 // TS
    cq, ck, cv = (lay[n] // FW for n in ("q_off", "k_off", "v_off"))
    ccol = jnp.swapaxes(c_new, 1, 2)[..., None]
    crow = jnp.swapaxes(c_new, 1, 2)[:, :, None, :]
    pt = page_table.reshape(-1)

    def new_spec(col):
        return pl.BlockSpec((TS, FW), lambda b, j, pt: (rb0 + b, col))

    def page_spec():
        return pl.BlockSpec((None, None, PAGE, FW), lambda b, j, pt: (pt[b * NP + NP - 1 - j], layer, 0, 0))

    grid_spec = pltpu.PrefetchScalarGridSpec(
        num_scalar_prefetch=1,
        grid=(DB, NP),
        in_specs=[
            new_spec(cq), new_spec(ck), new_spec(cv), page_spec(), page_spec(),
            pl.BlockSpec((None, None, FH, PAGE), lambda b, j, pt: (pt[b * NP + NP - 1 - j], layer, 0, 0)),
            pl.BlockSpec((None, FH, TS, 1), lambda b, j, pt: (b, 0, 0, 0)),
            pl.BlockSpec((None, FH, 1, TS), lambda b, j, pt: (b, 0, 0, 0)),
        ],
        out_specs=pl.BlockSpec((TS, FW), lambda b, j, pt: (b, 0)),
        scratch_shapes=[pltpu.VMEM((FH, TS, 1), F32), pltpu.VMEM((FH, TS, 1), F32),
                        pltpu.VMEM((TS, FW), F32), pltpu.VMEM((FH, 1), F32)],
    )
    return pl.pallas_call(
        functools.partial(_fox_decode_kernel, FH=FH, Dh=Dh, scale=Dh ** -0.5),
        grid_spec=grid_spec,
        out_shape=jax.ShapeDtypeStruct((DB * TS, FW), F32),
        compiler_params=_params(("parallel", "arbitrary"), 48),
        name="fox_decode",
    )(pt, proj, proj, proj, cache_k4, cache_v4, cache_lft, ccol, crow)


def _convglu_kernel(u_ref, uh_ref, up_ref, c0_ref, cw_ref, cb_ref, o_ref):
    first = pl.program_id(1) == 0
    u = u_ref[...]
    p2 = jnp.where(first, c0_ref[0:1, :], uh_ref[SUBLANES - 2:SUBLANES - 1, :])
    p1 = jnp.where(first, c0_ref[1:2, :], uh_ref[SUBLANES - 1:SUBLANES, :])
    u1 = _shift_rows(u, [p1])
    u2 = _shift_rows(u, [p2, p1])
    c = cb_ref[...] + cw_ref[0:1, :] * u2 + cw_ref[1:2, :] * u1 + cw_ref[2:3, :] * u
    act = 0.5 * c * (1.0 + jnp.tanh(0.7978845608028654 * (c + 0.044715 * c * c * c)))
    o_ref[...] = (act * up_ref[...]).astype(o_ref.dtype)


def convglu(u, up, grp, conv0, conv_w, conv_b, out_rows):
    F = u.shape[1]
    n_seq, T, row0 = grp
    R = _pick_tile(T, 2064, SUBLANES)
    cb = _pick_tile(F, 256, LANES)
    nR, nC = T // R, F // cb
    assert row0 % R == 0 and conv_w.shape[0] == 3
    rb0, hb0 = row0 // R, row0 // SUBLANES
    main = pl.BlockSpec((R, cb), lambda s, i, c: (rb0 + s * nR + i, c))
    halo = pl.BlockSpec((SUBLANES, cb),
                        lambda s, i, c: (jnp.maximum(hb0 + (s * T + i * R) // SUBLANES - 1, 0), c))
    in_specs = [main, halo, main,
                pl.BlockSpec((None, 2, cb), lambda s, i, c: (s, 0, c)),
                pl.BlockSpec((3, cb), lambda s, i, c: (0, c)),
                pl.BlockSpec((1, cb), lambda s, i, c: (0, c))]
    return pl.pallas_call(
        _convglu_kernel,
        grid=(n_seq, nR, nC),
        in_specs=in_specs,
        out_specs=pl.BlockSpec((R, cb), lambda s, i, c: (s * nR + i, c)),
        out_shape=jax.ShapeDtypeStruct((out_rows, F), BF16 if T % 16 == 0 else F32),
        compiler_params=_params(("parallel", "parallel", "parallel"), 48),
        name="convglu",
    )(u, u, up, conv0, conv_w, conv_b.reshape(1, F))


def _proj_layout(D, FH, Dh, RH, RN, DL, AL, GL):
    FW, RW = FH * Dh, RH * RN
    lay = {"FH": FH, "Dh": Dh, "RW": RW, "FW": FW}
    lay["q_off"], lay["k_off"], lay["v_off"] = 0, FW, 2 * FW
    lay["r_off"], lay["rk_off"], lay["rv_off"] = 3 * FW, 3 * FW + RW, 3 * FW + 2 * RW
    lay["ga_off"] = 3 * FW + 3 * RW
    lay["gb_off"] = lay["ga_off"] + D
    TW = _round_up(GL + DL + AL + FH, LANES)
    lay["TW"] = TW
    lay["t_off"] = _round_up(lay["gb_off"] + D, TW)
    lay["gd"], lay["wd"], lay["ad"], lay["ff"] = 0, GL, GL + DL, GL + DL + AL
    lay["NPJ"] = _round_up(lay["t_off"] + TW, 512)
    return lay


def _permute_cols(w, lay, D, DL, AL, GL):
    FW, RW, FH = lay["FW"], lay["RW"], lay["FH"]
    fox_end = 3 * FW + FH
    rkv = w[..., fox_end:fox_end + 3 * RW]
    wd = w[..., fox_end + 3 * RW:fox_end + 3 * RW + DL]
    ad = w[..., fox_end + 3 * RW + DL:fox_end + 3 * RW + DL + AL]
    gd = w[..., fox_end + 3 * RW + DL + AL:fox_end + 3 * RW + DL + AL + GL]
    gates = w[..., fox_end + 3 * RW + DL + AL + GL:]
    ff = w[..., 3 * FW:fox_end]
    lead = w.shape[:-1]
    pad1 = jnp.zeros(lead + (lay["t_off"] - (lay["gb_off"] + D),), w.dtype)
    pad2 = jnp.zeros(lead + (lay["NPJ"] - lay["t_off"] - (GL + DL + AL + FH),), w.dtype)
    return jnp.concatenate([w[..., :3 * FW], rkv, gates, pad1, gd, wd, ad, ff, pad2], axis=-1)


def _pad_lora(w_up, off, TW):
    L, RW = w_up.shape
    return jnp.zeros((TW, RW), BF16).at[off:off + L].set(w_up.astype(BF16))


def kernel(x_prompt, x_sample, cache_k, cache_v, cache_logf, state_wkv, state_shift, state_conv, page_table,
           meta_tokens, norm_mix_pre, norm_mix_post, norm_ffn_pre, norm_ffn_post, w_in, fox_forget_bias,
           rwkv_mu, rwkv_w0, rwkv_w_up, rwkv_a0, rwkv_a_up, rwkv_g_up, rwkv_k_k, rwkv_k_a, rwkv_r_k,
           rwkv_gn_g, rwkv_gn_b, w_o_fox, w_o_rwkv, w_out, ffn_w_gate, ffn_w_up, ffn_conv_w, ffn_conv_b,
           ffn_w_down):
    B, SEQ, D = x_prompt.shape
    DB, TS, _ = x_sample.shape
    n_pool, DEPTH, PAGE, FH, Dh = cache_k.shape
    NM = meta_tokens.shape[0]
    T = NM + SEQ
    RH, RN = rwkv_r_k.shape[1:]
    DL, AL, GL = rwkv_w_up.shape[1], rwkv_a_up.shape[1], rwkv_g_up.shape[1]
    F = ffn_w_gate.shape[2]
    CW = ffn_conv_w.shape[1]
    FW, RW = FH * Dh, RH * RN
    RP = 3 * RW + DL + AL + GL
    assert RN == CHUNK and FW == RW and CW == 3 and T % 16 == 0
    MP = B * T
    M = MP + DB * TS
    lay = _proj_layout(D, FH, Dh, RH, RN, DL, AL, GL)
    TW = lay["TW"]
    grp_p = (B, T, 0)
    grp_s = (DB, TS, MP)

    meta = jnp.broadcast_to(meta_tokens[None].astype(x_prompt.dtype), (B, NM, D))
    x = jnp.concatenate([jnp.concatenate([meta, x_prompt], axis=1).reshape(MP, D),
                         x_sample.reshape(DB * TS, D)], axis=0)

    cache_k4 = cache_k.reshape(n_pool, DEPTH, PAGE, FW)
    cache_v4 = cache_v.reshape(n_pool, DEPTH, PAGE, FW)
    cache_lft = jnp.swapaxes(cache_logf, 2, 3)

    def split_rows(a):
        return a[:MP].reshape(B, T, -1), a[MP:].reshape(DB, TS, -1)

    def join_rows(big, small):
        return lax.dynamic_update_slice(big, small.astype(big.dtype), (MP, 0))

    def orig_rwkv_cols(p):
        t = p[..., lay["t_off"]:]
        return jnp.concatenate([p[..., lay["r_off"]:lay["r_off"] + 3 * RW], t[..., lay["wd"]:lay["wd"] + DL],
                                t[..., lay["ad"]:lay["ad"] + AL], t[..., lay["gd"]:lay["gd"] + GL]], axis=-1)

    def rwkv_cols_split(v):
        tail = jnp.zeros(v.shape[:-1] + (TW,), v.dtype)
        tail = tail.at[..., lay["gd"]:lay["gd"] + GL].set(v[..., 3 * RW + DL + AL:])
        tail = tail.at[..., lay["wd"]:lay["wd"] + DL].set(v[..., 3 * RW:3 * RW + DL])
        tail = tail.at[..., lay["ad"]:lay["ad"] + AL].set(v[..., 3 * RW + DL:3 * RW + DL + AL])
        return v[..., :RW], v[..., RW:2 * RW], v[..., 2 * RW:3 * RW], tail

    h = rmsnorm_bf16(x, norm_mix_pre[0])
    rows_p, rows_s = [], []
    for l in range(DEPTH):
        w_in_l = _permute_cols(w_in[l], lay, D, DL, AL, GL).astype(BF16)
        proj = matmul(h, w_in_l)

        ff_p, ff_s = split_rows(proj[:, lay["t_off"] + lay["ff"]:lay["t_off"] + lay["ff"] + FH])
        lf_p, c_p = fox_lf(ff_p, fox_forget_bias[l])
        lf_s, c_s = fox_lf(ff_s, fox_forget_bias[l])
        fox = join_rows(fox_prompt(proj, lay, B, T, c_p, M),
                        fox_decode(proj, lay, cache_k4, cache_v4, cache_lft, page_table, l, c_s, MP))

        mu = [m.reshape(1, -1) for m in rwkv_cols_split(rwkv_mu[l])]
        lora = (_pad_lora(rwkv_w_up[l], lay["wd"], TW), _pad_lora(rwkv_a_up[l], lay["ad"], TW),
                _pad_lora(rwkv_g_up[l], lay["gd"], TW))
        vecs = [v.reshape(1, RW) for v in (rwkv_w0[l], rwkv_a0[l], rwkv_k_k[l], rwkv_k_a[l])]
        sh_p = [s[:, None, :] for s in rwkv_cols_split(jnp.zeros((B, RP), F32))]
        sh_s = [s[:, None, :] for s in rwkv_cols_split(state_shift[:, l])]
        prep_p = rwkv_prep(proj, lay, grp_p, sh_p, mu, *vecs, *lora)
        prep_s = rwkv_prep(proj, lay, grp_s, sh_s, mu, *vecs, *lora)
        s0_p = jnp.zeros((B, RW // LANES, LANES, CHUNK), F32)
        s0_s = jnp.swapaxes(state_wkv[:, l], 2, 3).reshape(DB, RW // LANES, LANES, CHUNK)
        scan_vecs = (rwkv_r_k[l].reshape(1, RW), rwkv_gn_g[l].reshape(1, RW), rwkv_gn_b[l].reshape(1, RW))
        rwkv_p, st_p = rwkv_scan(prep_p, B, T, s0_p, *scan_vecs, NM, M)
        rwkv_s, st_s = rwkv_scan(prep_s, DB, TS, s0_s, *scan_vecs, TS, DB * TS)
        rwkv = join_rows(rwkv_p, rwkv_s)

        def unpack_state(st):
            n = st.shape[0]
            st = st.reshape(n, RW // LANES, 2, CHUNK, 2, CHUNK)
            d = jnp.stack([st[:, :, 0, :, 0, :], st[:, :, 1, :, 1, :]], axis=2)
            return jnp.swapaxes(d.reshape(n, RH, CHUNK, CHUNK), 2, 3)

        mixed = gated_merge(fox, w_o_fox[l].astype(BF16), rwkv, w_o_rwkv[l].astype(BF16), proj,
                            lay["ga_off"], lay["gb_off"])
        y = matmul(mixed, w_out[l].astype(BF16))
        x, h2 = resnorm(y, x, norm_mix_post[l], norm_ffn_pre[l])

        u = matmul(h2, ffn_w_gate[l].astype(BF16))
        up = matmul(h2, ffn_w_up[l].astype(BF16))
        act = join_rows(convglu(u, up, grp_p, jnp.zeros((B, CW - 1, F), F32), ffn_conv_w[l], ffn_conv_b[l], M),
                        convglu(u, up, grp_s, state_conv[:, l], ffn_conv_w[l], ffn_conv_b[l], DB * TS))
        f = matmul(act, ffn_w_down[l].astype(BF16))
        x, h = resnorm(f, x, norm_ffn_post[l], norm_mix_pre[l + 1] if l + 1 < DEPTH else None)

        k_p, k_s = split_rows(proj[:, lay["k_off"]:lay["k_off"] + FW])
        v_p, v_s = split_rows(proj[:, lay["v_off"]:lay["v_off"] + FW])
        u_p, u_s = split_rows(u)
        sh_new_p = orig_rwkv_cols(proj[:MP].reshape(B, T, -1)[:, -1])
        sh_new_s = orig_rwkv_cols(proj[MP:].reshape(DB, TS, -1)[:, -1])
        rows_p.append((k_p.reshape(B, T, FH, Dh), v_p.reshape(B, T, FH, Dh), lf_p, unpack_state(st_p),
                       sh_new_p, u_p[:, T - (CW - 1):]))
        rows_s.append((k_s.reshape(DB, TS, FH, Dh), v_s.reshape(DB, TS, FH, Dh), lf_s, unpack_state(st_s),
                       sh_new_s, u_s[:, TS - (CW - 1):]))

    def stack(rows, i):
        return jnp.stack([r[i] for r in rows], axis=1)

    y_prompt = x[:MP].reshape(B, T, D)[:, NM:]
    y_sample = x[MP:].reshape(DB, TS, D)
    return (y_prompt, y_sample,
            *(stack(rows_p, i) for i in range(6)),
            *(stack(rows_s, i) for i in range(6)))
```

```python
import functools

import jax
import jax.numpy as jnp
from jax import lax
from jax.experimental import pallas as pl
from jax.experimental.pallas import tpu as pltpu

F32 = jnp.float32
BF16 = jnp.bfloat16
HIGHEST = lax.Precision.HIGHEST

NORM_EPS = 1e-6
GN_EPS = 64e-5
NEG_INF = -1e30

LANES = 128
SUBLANES = 8
CHUNK = 64
MIB = 2 ** 20


def _round_up(x, m):
    return (x + m - 1) // m * m


def _pick_tile(n, target, align):
    best = None
    for d in range(align, min(n, target) + 1, align):
        if n % d == 0:
            best = d
    return n if best is None else best


def _params(sem, vmem_mib):
    return pltpu.CompilerParams(dimension_semantics=sem, vmem_limit_bytes=int(vmem_mib * MIB))


def _rms(x, g):
    return x * lax.rsqrt(jnp.mean(x * x, axis=-1, keepdims=True) + NORM_EPS) * g


def _rmsnorm_kernel(x_ref, g_ref, o_ref):
    o_ref[...] = _rms(x_ref[...], g_ref[...]).astype(o_ref.dtype)


def rmsnorm_bf16(x, g):
    M, D = x.shape
    tr = _pick_tile(M, 512, 16)
    return pl.pallas_call(
        _rmsnorm_kernel,
        grid=(M // tr,),
        in_specs=[pl.BlockSpec((tr, D), lambda i: (i, 0)), pl.BlockSpec((1, D), lambda i: (0, 0))],
        out_specs=pl.BlockSpec((tr, D), lambda i: (i, 0)),
        out_shape=jax.ShapeDtypeStruct((M, D), BF16),
        compiler_params=_params(("parallel",), 48),
        name="rmsnorm",
    )(x, g.reshape(1, D))


def _resnorm_kernel(y_ref, x_ref, gp_ref, gn_ref, xo_ref, ho_ref):
    xn = x_ref[...] + _rms(y_ref[...], gp_ref[...])
    xo_ref[...] = xn
    ho_ref[...] = _rms(xn, gn_ref[...]).astype(ho_ref.dtype)


def _resnorm_last_kernel(y_ref, x_ref, gp_ref, xo_ref):
    xo_ref[...] = x_ref[...] + _rms(y_ref[...], gp_ref[...])


def resnorm(y, x, g_post, g_next):
    M, D = x.shape
    tr = _pick_tile(M, 512, 16)
    row = pl.BlockSpec((tr, D), lambda i: (i, 0))
    vec = pl.BlockSpec((1, D), lambda i: (0, 0))
    if g_next is None:
        return pl.pallas_call(
            _resnorm_last_kernel, grid=(M // tr,), in_specs=[row, row, vec], out_specs=row,
            out_shape=jax.ShapeDtypeStruct((M, D), F32),
            compiler_params=_params(("parallel",), 48), name="resnorm_last",
        )(y, x, g_post.reshape(1, D)), None
    return pl.pallas_call(
        _resnorm_kernel, grid=(M // tr,), in_specs=[row, row, vec, vec], out_specs=[row, row],
        out_shape=[jax.ShapeDtypeStruct((M, D), F32), jax.ShapeDtypeStruct((M, D), BF16)],
        compiler_params=_params(("parallel",), 56), name="resnorm",
    )(y, x, g_post.reshape(1, D), g_next.reshape(1, D))


def _mm_kernel(a_ref, w_ref, o_ref):
    o_ref[...] = jnp.dot(a_ref[...], w_ref[...], preferred_element_type=F32).astype(o_ref.dtype)


def _mm_acc_kernel(a_ref, w_ref, o_ref, acc_ref):
    k = pl.program_id(2)

    @pl.when(k == 0)
    def _():
        acc_ref[...] = jnp.zeros_like(acc_ref)

    acc_ref[...] += jnp.dot(a_ref[...], w_ref[...], preferred_element_type=F32)

    @pl.when(k == pl.num_programs(2) - 1)
    def _():
        o_ref[...] = acc_ref[...].astype(o_ref.dtype)


def matmul(a, w, out_dtype=F32, tm_target=1024, tn_target=512, tk_target=6144):
    M, K = a.shape
    N = w.shape[1]
    tm = _pick_tile(M, tm_target, 16)
    tn = _pick_tile(N, tn_target, LANES)
    tk = _pick_tile(K, tk_target, LANES)
    nk = K // tk
    osz = jnp.dtype(out_dtype).itemsize
    vmem = 2 * (tm * tk * 2 + tk * tn * 2 + tm * tn * osz) + tm * tn * 4
    vmem_mib = vmem / MIB + 8
    if nk == 1:
        return pl.pallas_call(
            _mm_kernel,
            grid=(M // tm, N // tn),
            in_specs=[pl.BlockSpec((tm, K), lambda i, j: (i, 0)), pl.BlockSpec((K, tn), lambda i, j: (0, j))],
            out_specs=pl.BlockSpec((tm, tn), lambda i, j: (i, j)),
            out_shape=jax.ShapeDtypeStruct((M, N), out_dtype),
            compiler_params=_params(("parallel", "parallel"), vmem_mib),
            name="matmul",
        )(a, w)
    return pl.pallas_call(
        _mm_acc_kernel,
        grid=(M // tm, N // tn, nk),
        in_specs=[pl.BlockSpec((tm, tk), lambda i, j, k: (i, k)), pl.BlockSpec((tk, tn), lambda i, j, k: (k, j))],
        out_specs=pl.BlockSpec((tm, tn), lambda i, j, k: (i, j)),
        out_shape=jax.ShapeDtypeStruct((M, N), out_dtype),
        scratch_shapes=[pltpu.VMEM((tm, tn), F32)],
        compiler_params=_params(("parallel", "parallel", "arbitrary"), vmem_mib),
        name="matmul_acc",
    )(a, w)


def _merge_kernel(a1_ref, w1_ref, a2_ref, w2_ref, ga_ref, gb_ref, o_ref):
    f = jnp.dot(a1_ref[...], w1_ref[...], preferred_element_type=F32)
    r = jnp.dot(a2_ref[...], w2_ref[...], preferred_element_type=F32)
    o_ref[...] = (jax.nn.sigmoid(ga_ref[...]) * f + jax.nn.sigmoid(gb_ref[...]) * r).astype(o_ref.dtype)


def gated_merge(fox, w_of, rwkv, w_or, proj, ga_off, gb_off):
    M, K1 = fox.shape
    K2 = rwkv.shape[1]
    D = w_of.shape[1]
    tm = _pick_tile(M, 1024, 16)
    tn = _pick_tile(D, 512, LANES)
    assert ga_off % tn == 0 and gb_off % tn == 0
    ja, jb = ga_off // tn, gb_off // tn
    vmem = 2 * (tm * (K1 + K2) * 2 + (K1 + K2) * tn * 2 + 2 * tm * tn * 4 + tm * tn * 2) + 2 * tm * tn * 4
    return pl.pallas_call(
        _merge_kernel,
        grid=(M // tm, D // tn),
        in_specs=[
            pl.BlockSpec((tm, K1), lambda i, j: (i, 0)), pl.BlockSpec((K1, tn), lambda i, j: (0, j)),
            pl.BlockSpec((tm, K2), lambda i, j: (i, 0)), pl.BlockSpec((K2, tn), lambda i, j: (0, j)),
            pl.BlockSpec((tm, tn), lambda i, j: (i, ja + j)), pl.BlockSpec((tm, tn), lambda i, j: (i, jb + j)),
        ],
        out_specs=pl.BlockSpec((tm, tn), lambda i, j: (i, j)),
        out_shape=jax.ShapeDtypeStruct((M, D), BF16),
        compiler_params=_params(("parallel", "parallel"), vmem / MIB + 8),
        name="gated_merge",
    )(fox, w_of, rwkv, w_or, proj, proj)


def _shift_rows(x, fill_rows):
    n = len(fill_rows)
    out = pltpu.roll(x, n, axis=0)
    row = lax.broadcasted_iota(jnp.int32, x.shape, 0)
    for i, f in enumerate(fill_rows):
        out = jnp.where(row == i, f, out)
    return out


def _rwkv_prep_kernel(r_ref, k_ref, v_ref, t_ref, rh_ref, kh_ref, vh_ref, th_ref,
                      r0_ref, k0_ref, v0_ref, t0_ref, mur_ref, muk_ref, muv_ref, mut_ref,
                      w0_ref, a0_ref, kk_ref, ka_ref, wup_ref, aup_ref, gup_ref,
                      ro_ref, ld_ref, ko_ref, vo_ref, ao_ref, bo_ref, go_ref):
    first = pl.program_id(1) == 0

    def tokshift(x_ref, h_ref, s_ref, mu_ref):
        x = x_ref[...]
        prev0 = jnp.where(first, s_ref[...], h_ref[SUBLANES - 1:SUBLANES, :])
        prev = _shift_rows(x, [prev0])
        return x + (prev - x) * mu_ref[...]

    r = tokshift(r_ref, rh_ref, r0_ref, mur_ref)
    k = tokshift(k_ref, kh_ref, k0_ref, muk_ref)
    v = tokshift(v_ref, vh_ref, v0_ref, muv_ref)
    t = tokshift(t_ref, th_ref, t0_ref, mut_ref)

    wlin = w0_ref[...] + jnp.dot(jnp.tanh(t).astype(BF16), wup_ref[...], preferred_element_type=F32)
    w = -jax.nn.softplus(-wlin) - 0.5
    a = jax.nn.sigmoid(a0_ref[...] + jnp.dot(t.astype(BF16), aup_ref[...], preferred_element_type=F32))
    g = jnp.dot(jax.nn.sigmoid(t).astype(BF16), gup_ref[...], preferred_element_type=F32)

    kkp = k * kk_ref[...]
    sq = kkp * kkp
    cw = sq.shape[1]
    row = lax.broadcasted_iota(jnp.int32, (LANES, LANES), 0)
    col = lax.broadcasted_iota(jnp.int32, (LANES, LANES), 1)
    ones_bd = jnp.where(row // CHUNK == col // CHUNK, 1.0, 0.0).astype(F32)
    parts = [jnp.dot(sq[:, c:c + LANES], ones_bd, precision=HIGHEST, preferred_element_type=F32)
             for c in range(0, cw, LANES)]
    nrm = jnp.sqrt(jnp.concatenate(parts, axis=1) if len(parts) > 1 else parts[0])
    kk = kkp / jnp.maximum(nrm, 1e-12)

    ro_ref[...] = r.astype(ro_ref.dtype)
    ld_ref[...] = -jnp.exp(w)
    ko_ref[...] = (k * (1.0 + (a - 1.0) * ka_ref[...])).astype(ko_ref.dtype)
    vo_ref[...] = v.astype(vo_ref.dtype)
    ao_ref[...] = (-kk).astype(ao_ref.dtype)
    bo_ref[...] = (kk * a).astype(bo_ref.dtype)
    go_ref[...] = g.astype(go_ref.dtype)


def rwkv_prep(proj, lay, grp, shift0, mu, w0, a0, k_k, k_a, wup, aup, gup):
    RW, TW = lay["RW"], lay["TW"]
    n_seq, T, row0 = grp
    R = _pick_tile(T, 1024, 16)
    cb = _pick_tile(RW, 512, LANES)
    nR, nC = T // R, RW // cb
    rb0 = row0 // R
    hb0 = row0 // SUBLANES
    assert row0 % R == 0 and lay["r_off"] % cb == 0 and lay["t_off"] % TW == 0
    cr, ck, cv = (lay[n] // cb for n in ("r_off", "rk_off", "rv_off"))
    ct = lay["t_off"] // TW

    def main(c0):
        return pl.BlockSpec((R, cb), lambda s, i, c: (rb0 + s * nR + i, c0 + c))

    def halo(c0):
        return pl.BlockSpec((SUBLANES, cb),
                            lambda s, i, c: (jnp.maximum(hb0 + (s * T + i * R) // SUBLANES - 1, 0), c0 + c))

    tail = pl.BlockSpec((R, TW), lambda s, i, c: (rb0 + s * nR + i, ct))
    tail_h = pl.BlockSpec((SUBLANES, TW),
                          lambda s, i, c: (jnp.maximum(hb0 + (s * T + i * R) // SUBLANES - 1, 0), ct))
    s0 = pl.BlockSpec((None, 1, cb), lambda s, i, c: (s, 0, c))
    s0t = pl.BlockSpec((None, 1, TW), lambda s, i, c: (s, 0, 0))
    vec = pl.BlockSpec((1, cb), lambda s, i, c: (0, c))
    vect = pl.BlockSpec((1, TW), lambda s, i, c: (0, 0))
    lora = pl.BlockSpec((TW, cb), lambda s, i, c: (0, c))
    out = pl.BlockSpec((R, cb), lambda s, i, c: (s * nR + i, c))
    in_specs = [main(cr), main(ck), main(cv), tail, halo(cr), halo(ck), halo(cv), tail_h,
                s0, s0, s0, s0t, vec, vec, vec, vect, vec, vec, vec, vec, lora, lora, lora]
    args = [proj, proj, proj, proj, proj, proj, proj, proj, *shift0, *mu, w0, a0, k_k, k_a, wup, aup, gup]
    return pl.pallas_call(
        _rwkv_prep_kernel,
        grid=(n_seq, nR, nC),
        in_specs=in_specs,
        out_specs=[out] * 7,
        out_shape=[jax.ShapeDtypeStruct((n_seq * T, RW), F32 if (i == 1 or R % 16) else BF16) for i in range(7)],
        compiler_params=_params(("parallel", "parallel", "parallel"), 48),
        name="rwkv_prep",
    )(*args)


def _seg_sum(x, lo):
    s0 = jnp.sum(jnp.where(lo, x, 0.0), axis=1, keepdims=True)
    s1 = jnp.sum(jnp.where(lo, 0.0, x), axis=1, keepdims=True)
    return jnp.where(lo, s0, s1)


def _rwkv_scan_kernel(r_ref, ld_ref, k_ref, v_ref, a_ref, b_ref, g_ref, s0_ref, rk_ref, gng_ref, gnb_ref,
                      o_ref, so_ref, st_ref, *, n_chunks, n0, rows, n_dbl, pairs):
    C = CHUNK
    row = lax.broadcasted_iota(jnp.int32, (C, LANES), 0)
    lane = lax.broadcasted_iota(jnp.int32, (C, LANES), 1)
    lane_in = lane % C
    lo = lane < C
    m_strict = lane_in < row
    m_incl = lane_in <= row
    eye_pair = jnp.where(lane_in == row, 1.0, 0.0).astype(F32)
    row2 = lax.broadcasted_iota(jnp.int32, (2 * C, LANES), 0)
    lane2 = lax.broadcasted_iota(jnp.int32, (2 * C, LANES), 1)
    bdm = (row2 // C) == (lane2 // C)
    eye2 = row2 == lane2
    tr = lax.broadcasted_iota(jnp.int32, (C, C), 0)
    tc = lax.broadcasted_iota(jnp.int32, (C, C), 1)
    tril = jnp.where(tc <= tr, 1.0, 0.0).astype(F32)

    def bd(x):
        return jnp.where(bdm, jnp.concatenate([x, x], axis=0), 0.0)

    def mm(x, y):
        return jnp.dot(x.astype(BF16), y.astype(BF16), preferred_element_type=F32)

    def mm_tn(x, y):
        return mm(x.T, y)

    def mm_nt(x, y):
        return lax.dot_general(x.astype(BF16), y.astype(BF16), (((1,), (1,)), ((), ())),
                               preferred_element_type=F32)

    for p in range(pairs):
        st_ref[p] = jnp.where(bdm, jnp.concatenate([s0_ref[p], s0_ref[p]], axis=1), 0.0)

    def chunk(c, carry):
        if n_chunks == 1:
            start = 0
            nact = n0
        else:
            start = pl.multiple_of(jnp.where(c == 0, 0, n0 + (c - 1) * C), 16)
            nact = jnp.where(c == 0, n0, C)
        act = row < nact

        lss = [slice(p * LANES, (p + 1) * LANES) for p in range(pairs)]

        def each(f, *cols):
            return [f(*xs) for xs in zip(*cols)]

        def load(ref):
            def one(ls):
                if rows >= C:
                    return ref[pl.ds(start, C), ls].astype(F32)
                x = ref[:, ls].astype(F32)
                return jnp.concatenate([x, jnp.zeros((C - rows, LANES), F32)], axis=0)
            return [one(ls) for ls in lss]

        def masked(xs):
            return [jnp.where(act, x, 0.0) for x in xs]

        r_raw, k_raw, v_raw = load(r_ref), load(k_ref), load(v_ref)
        logd = masked(load(ld_ref))
        r, k, v = masked(r_raw), masked(k_raw), masked(v_raw)
        a, b = masked(load(a_ref)), masked(load(b_ref))

        lw = each(lambda x: jnp.dot(tril, x, precision=HIGHEST, preferred_element_type=F32), logd)
        lw_end = each(lambda x: x[C - 1:C, :], lw)
        at = each(lambda x, l, d: x * jnp.exp(l - d), a, lw, logd)
        rt = each(lambda x, l: x * jnp.exp(l), r, lw)
        dec_out = each(lambda l: jnp.exp(-l), lw)
        bt = each(jnp.multiply, b, dec_out)
        kt = each(jnp.multiply, k, dec_out)
        dec_tail = each(lambda e, l: jnp.exp(e - l), lw_end, lw)
        btl = each(jnp.multiply, b, dec_tail)
        ktl = each(jnp.multiply, k, dec_tail)

        big = each(lambda x, y, z, w: mm_nt(jnp.concatenate([x, y], axis=0),
                                            jnp.concatenate([bd(z), bd(w)], axis=0)), at, rt, bt, kt)
        a_ab = each(lambda x: jnp.where(m_strict, x[0:C, 0:LANES], 0.0), big)
        a_ak = each(lambda x: jnp.where(m_strict, x[0:C, LANES:2 * LANES], 0.0), big)
        a_rb = each(lambda x: jnp.where(m_incl, x[C:2 * C, 0:LANES], 0.0), big)
        a_rk = each(lambda x: jnp.where(m_incl, x[C:2 * C, LANES:2 * LANES], 0.0), big)

        pw = a_ab
        x = each(lambda m: eye_pair + m, a_ab)
        for _ in range(n_dbl):
            pw = each(lambda m: mm(m, bd(m)), pw)
            x = each(lambda xm, m: xm + mm(xm, bd(m)), x, pw)

        av_op = each(lambda m, n, vv: mm(jnp.concatenate([m, n], axis=0), bd(vv)), a_ak, a_rk, v)
        z = each(lambda xm, m, n: mm(xm, jnp.concatenate([bd(m), bd(n[0:C])], axis=1)), x, at, av_op)
        y = each(lambda m, zz: mm(m, jnp.concatenate([bd(zz[:, 0:LANES]), bd(zz[:, LANES:2 * LANES])], axis=1)),
                 a_rb, z)
        gmat = each(lambda m, yy: m + yy[:, 0:LANES], rt, y)
        hmat = each(lambda m, yy: m[C:2 * C] + yy[:, LANES:2 * LANES], av_op, y)
        m_mat = each(lambda e, m, zz: jnp.where(eye2, jnp.broadcast_to(jnp.exp(e), (2 * C, LANES)), 0.0)
                     + jnp.where(bdm, mm_tn(m, zz[:, 0:LANES]), 0.0), lw_end, btl, z)
        n_mat = each(lambda m, n, zz, vv: jnp.where(
            bdm, mm_tn(jnp.concatenate([m, n], axis=0),
                       jnp.concatenate([zz[:, LANES:2 * LANES], vv], axis=0)), 0.0), btl, ktl, z, v)

        st = [st_ref[p] for p in range(pairs)]
        o = each(lambda gm, s, hm: mm(gm, s) + hm, gmat, st, hmat)
        st_new = each(lambda mmat, s, nm: mm(mmat, s) + nm, m_mat, st, n_mat)
        for p in range(pairs):
            st_ref[p] = st_new[p]

        oc = each(lambda m: m - _seg_sum(m, lo) * (1.0 / C), o)
        var = each(lambda m: _seg_sum(m * m, lo) * (1.0 / C), oc)
        on = each(lambda m, vv, ls: m * lax.rsqrt(vv + GN_EPS) * gng_ref[:, ls] + gnb_ref[:, ls], oc, var, lss)
        bonus = each(lambda rr, kk, vv, ls: _seg_sum(rr * kk * rk_ref[:, ls], lo) * vv, r_raw, k_raw, v_raw, lss)
        res = each(lambda m, bo, g: ((m + bo) * g).astype(o_ref.dtype), on, bonus, load(g_ref))
        for ls, rs in zip(lss, res):
            if rows >= C:
                o_ref[pl.ds(start, C), ls] = rs
            else:
                o_ref[:, ls] = rs[0:rows]
        return carry

    if n_chunks == 1:
        chunk(0, 0)
    else:
        lax.fori_loop(0, n_chunks, chunk, 0)
    for p in range(pairs):
        so_ref[p] = st_ref[p]


def rwkv_scan(prep, n_seq, T, s0t, r_k, gn_g, gn_b, n0, out_rows):
    RW = prep[0].shape[1]
    pairs = next(p for p in (4, 2, 1) if (RW // LANES) % p == 0)
    cb = pairs * LANES
    nP = RW // cb
    if T >= CHUNK:
        assert (T - n0) % CHUNK == 0 and n0 % 16 == 0
        n_chunks = 1 + (T - n0) // CHUNK
        nmax = CHUNK
    else:
        assert T == n0
        n_chunks = 1
        nmax = n0
    n_dbl = max(0, (nmax - 1).bit_length() - 1)
    seq = pl.BlockSpec((T, cb), lambda s, p: (s, p))
    vec = pl.BlockSpec((1, cb), lambda s, p: (0, p))
    st_in = pl.BlockSpec((None, pairs, LANES, CHUNK), lambda s, p: (s, p, 0, 0))
    st_out = pl.BlockSpec((None, pairs, LANES, LANES), lambda s, p: (s, p, 0, 0))
    out_dtype = BF16 if T % 16 == 0 else F32
    return pl.pallas_call(
        functools.partial(_rwkv_scan_kernel, n_chunks=n_chunks, n0=n0, rows=T, n_dbl=n_dbl, pairs=pairs),
        grid=(n_seq, nP),
        in_specs=[seq] * 7 + [st_in, vec, vec, vec],
        out_specs=[seq, st_out],
        out_shape=[jax.ShapeDtypeStruct((out_rows, RW), out_dtype),
                   jax.ShapeDtypeStruct((n_seq, RW // LANES, LANES, LANES), F32)],
        scratch_shapes=[pltpu.VMEM((pairs, LANES, LANES), F32)],
        compiler_params=_params(("parallel", "parallel"), 56),
        name="rwkv_scan",
    )(*prep, s0t, r_k, gn_g, gn_b)


def _fox_lf_kernel(ff_ref, bias_ref, lf_ref, c_ref, *, tb):
    T = ff_ref.shape[0]
    z = ff_ref[...] + bias_ref[...]
    lf = jnp.minimum(z, 0.0) - jnp.log1p(jnp.exp(-jnp.abs(z)))
    lf_ref[...] = lf
    for r0 in range(0, T, tb):
        row = lax.broadcasted_iota(jnp.int32, (tb, T), 0) + r0
        col = lax.broadcasted_iota(jnp.int32, (tb, T), 1)
        tri = jnp.where(col <= row, 1.0, 0.0).astype(F32)
        c_ref[r0:r0 + tb, :] = jnp.dot(tri, lf, precision=HIGHEST, preferred_element_type=F32)


def fox_lf(ff, bias):
    n_seq, T, H = ff.shape
    tb = _pick_tile(T, 512, SUBLANES)
    blk = pl.BlockSpec((None, T, H), lambda s: (s, 0, 0))
    return pl.pallas_call(
        functools.partial(_fox_lf_kernel, tb=tb),
        grid=(n_seq,),
        in_specs=[blk, pl.BlockSpec((1, H), lambda s: (0, 0))],
        out_specs=[blk, blk],
        out_shape=[jax.ShapeDtypeStruct((n_seq, T, H), F32)] * 2,
        compiler_params=_params(("parallel",), 48),
        name="fox_lf",
    )(ff, bias.reshape(1, H))


def _fox_prompt_kernel(q_ref, k_ref, v_ref, c_ref, ct_ref, o_ref, *, tq, scale):
    T, Dh = q_ref.shape
    nb = T // tq
    H = c_ref.shape[1]
    h = pl.program_id(1)
    hl = lax.broadcasted_iota(jnp.int32, (T, H), 1)
    ccol = jnp.sum(jnp.where(hl == h, c_ref[...], 0.0), axis=1, keepdims=True)
    ri = lax.broadcasted_iota(jnp.int32, (tq, tq), 0)
    ci = lax.broadcasted_iota(jnp.int32, (tq, tq), 1)

    for qi in range(nb):
        q = q_ref[qi * tq:(qi + 1) * tq, :].astype(BF16)
        cq = ccol[qi * tq:(qi + 1) * tq, :]

        def kv_step(kj, carry):
            m, l, acc = carry
            k0 = pl.multiple_of(kj * tq, SUBLANES)
            kb = k_ref[pl.ds(k0, tq), :].astype(BF16)
            vb = v_ref[pl.ds(k0, tq), :].astype(BF16)
            s = lax.dot_general(q, kb, (((1,), (1,)), ((), ())), preferred_element_type=F32) * scale
            s = s + (cq - ct_ref[h, pl.ds(kj, 1), :])
            s = jnp.where(ci + (kj - qi) * tq <= ri, s, NEG_INF)
            m_new = jnp.maximum(m, jnp.max(s, axis=1, keepdims=True))
            alpha = jnp.exp(m - m_new)
            p = jnp.exp(s - m_new)
            l = alpha * l + jnp.sum(p, axis=1, keepdims=True)
            acc = alpha * acc + jnp.dot(p.astype(BF16), vb, preferred_element_type=F32)
            return m_new, l, acc

        init = (jnp.full((tq, 1), NEG_INF, F32), jnp.zeros((tq, 1), F32), jnp.zeros((tq, Dh), F32))
        m, l, acc = lax.fori_loop(0, qi + 1, kv_step, init)
        o_ref[qi * tq:(qi + 1) * tq, :] = (acc / l).astype(o_ref.dtype)


def fox_prompt(proj, lay, B, T, c, out_rows):
    FH, Dh = lay["FH"], lay["Dh"]
    tq = _pick_tile(T, 512, SUBLANES)
    nb = T // tq
    ct = jnp.swapaxes(c, 1, 2).reshape(B, FH, nb, tq)
    assert Dh % LANES == 0 and lay["q_off"] % Dh == 0
    cq, ck, cv = (lay[n] // Dh for n in ("q_off", "k_off", "v_off"))
    return pl.pallas_call(
        functools.partial(_fox_prompt_kernel, tq=tq, scale=Dh ** -0.5),
        grid=(B, FH),
        in_specs=[
            pl.BlockSpec((T, Dh), lambda b, h: (b, cq + h)),
            pl.BlockSpec((T, Dh), lambda b, h: (b, ck + h)),
            pl.BlockSpec((T, Dh), lambda b, h: (b, cv + h)),
            pl.BlockSpec((None, T, FH), lambda b, h: (b, 0, 0)),
            pl.BlockSpec((None, FH, nb, tq), lambda b, h: (b, 0, 0, 0)),
        ],
        out_specs=pl.BlockSpec((T, Dh), lambda b, h: (b, h)),
        out_shape=jax.ShapeDtypeStruct((out_rows, FH * Dh), BF16),
        compiler_params=_params(("parallel", "parallel"), 48),
        name="fox_prompt",
    )(proj, proj, proj, c, ct)


def _page_sums_kernel(lf_ref, suf_ref, tot_ref, *, FH):
    x = lf_ref[...]
    hi = x.astype(BF16)
    lo = (x - hi.astype(F32)).astype(BF16)
    lo2 = (x - hi.astype(F32) - lo.astype(F32)).astype(BF16)
    nv = x.shape[1] // LANES
    r = lax.broadcasted_iota(jnp.int32, (LANES, 2 * LANES), 0)
    c = lax.broadcasted_iota(jnp.int32, (LANES, 2 * LANES), 1)
    same_head = (r % FH) == (c % FH)
    later = (r // FH) > ((c % LANES) // FH)
    sel = jnp.where(same_head & (later | (c >= LANES)), 1.0, 0.0).astype(BF16)
    carry = jnp.zeros((x.shape[0], LANES), F32)
    for v in reversed(range(nv)):
        ls = slice(v * LANES, (v + 1) * LANES)
        y = (jnp.dot(hi[:, ls], sel, preferred_element_type=F32)
             + jnp.dot(lo[:, ls], sel, preferred_element_type=F32)
             + jnp.dot(lo2[:, ls], sel, preferred_element_type=F32))
        suf_ref[:, ls] = y[:, 0:LANES] + carry
        carry = carry + y[:, LANES:2 * LANES]
    for v in range(nv):
        tot_ref[:, v * LANES:(v + 1) * LANES] = carry


def fox_page_sums(lf_rows, FH):
    n, W = lf_rows.shape
    assert LANES % FH == 0 and W % LANES == 0
    tr = _pick_tile(n, 256, SUBLANES)
    blk = pl.BlockSpec((tr, W), lambda i: (i, 0))
    return pl.pallas_call(
        functools.partial(_page_sums_kernel, FH=FH),
        grid=(n // tr,),
        in_specs=[blk],
        out_specs=[blk, blk],
        out_shape=[jax.ShapeDtypeStruct((n, W), F32)] * 2,
        compiler_params=_params(("parallel",), 48),
        name="fox_page_sums",
    )(lf_rows)


def _fox_decode_kernel(pt_ref, q_ref, kn_ref, vn_ref, *refs, FH, scale, G):
    kp_refs, vp_refs, suf_refs, tot_refs = (refs[i * G:(i + 1) * G] for i in range(4))
    ccol_ref, crow_ref, o_ref, m_ref, l_ref, acc_ref, r_ref, mask_ref = refs[4 * G:]
    j = pl.program_id(1)
    nj = pl.num_programs(1)
    R = q_ref.shape[0]

    @pl.when(j == 0)
    def _():
        m_ref[...] = jnp.full_like(m_ref, NEG_INF)
        l_ref[...] = jnp.zeros_like(l_ref)
        acc_ref[...] = jnp.zeros_like(acc_ref)
        r_ref[...] = jnp.zeros_like(r_ref)
        rr = lax.broadcasted_iota(jnp.int32, mask_ref.shape, 0)
        cc = lax.broadcasted_iota(jnp.int32, mask_ref.shape, 1)
        mask_ref[...] = jnp.where((rr % FH) == (cc % FH), 0.0, NEG_INF)

    q = q_ref[...].astype(BF16)
    ccol = ccol_ref[...]

    def update(ss, vbs):
        m_old = m_ref[...]
        m_new = m_old
        for s in ss:
            m_new = jnp.maximum(m_new, jnp.max(s, axis=1, keepdims=True))
        alpha = jnp.exp(m_old - m_new)
        ps = [jnp.exp(s - m_new) for s in ss]
        l_new = alpha * l_ref[...]
        acc = alpha * acc_ref[...]
        for p, vb in zip(ps, vbs):
            l_new = l_new + jnp.sum(p, axis=1, keepdims=True)
            acc = acc + jnp.dot(p.astype(BF16), vb, preferred_element_type=F32)
        l_ref[...] = l_new
        acc_ref[...] = acc
        m_ref[...] = m_new

    def scores(k_ref):
        return lax.dot_general(q, k_ref[...].astype(BF16), (((1,), (1,)), ((), ())),
                               preferred_element_type=F32) * scale

    ss = [scores(k_ref) for k_ref in kp_refs]
    later = r_ref[...]
    for g in range(G):
        ss[g] = ss[g] + (mask_ref[...] + (suf_refs[g][...] + later)) + ccol
        later = later + tot_refs[g][...]
    update(ss, [v_ref[...].astype(BF16) for v_ref in vp_refs])
    r_ref[...] = later

    @pl.when(j == nj - 1)
    def _():
        ri = lax.broadcasted_iota(jnp.int32, (R, R), 0)
        ci = lax.broadcasted_iota(jnp.int32, (R, R), 1)
        ok = ((ri % FH) == (ci % FH)) & ((ci // FH) <= (ri // FH))
        sn = jnp.where(ok, scores(kn_ref) + (ccol - crow_ref[...]), NEG_INF)
        update([sn], [vn_ref[...].astype(BF16)])
        o_ref[...] = (acc_ref[...] / l_ref[...]).astype(o_ref.dtype)


def fox_decode(q2, kn2, vn2, cache_k2, cache_v2, suf, tot, page_table, layer, c_new, FH):
    DB, R, Dh = q2.shape
    NP = page_table.shape[1]
    DEPTH, W = cache_k2.shape[1], cache_k2.shape[2]
    ccol = c_new.reshape(DB, R, 1)
    crow = c_new.reshape(DB, 1, R)
    pt = page_table.reshape(-1)
    G = next(g for g in (4, 2, 1) if NP % g == 0)
    new_spec = pl.BlockSpec((None, R, Dh), lambda b, j, pt: (b, 0, 0))

    def page(b, j, pt, g):
        return pt[b * NP + NP - 1 - (j * G + g)]

    page_specs = [pl.BlockSpec((None, None, W, Dh), lambda b, j, pt, g=g: (page(b, j, pt, g), layer, 0, 0))
                  for g in range(G)]
    row_specs = [pl.BlockSpec((None, 1, W), lambda b, j, pt, g=g: (page(b, j, pt, g) * DEPTH + layer, 0, 0))
                 for g in range(G)]
    grid_spec = pltpu.PrefetchScalarGridSpec(
        num_scalar_prefetch=1,
        grid=(DB, NP // G),
        in_specs=[new_spec, new_spec, new_spec, *page_specs, *page_specs, *row_specs, *row_specs,
                  pl.BlockSpec((None, R, 1), lambda b, j, pt: (b, 0, 0)),
                  pl.BlockSpec((None, 1, R), lambda b, j, pt: (b, 0, 0))],
        out_specs=pl.BlockSpec((None, R, Dh), lambda b, j, pt: (b, 0, 0)),
        scratch_shapes=[pltpu.VMEM((R, 1), F32), pltpu.VMEM((R, 1), F32), pltpu.VMEM((R, Dh), F32),
                        pltpu.VMEM((1, W), F32), pltpu.VMEM((R, W), F32)],
    )
    return pl.pallas_call(
        functools.partial(_fox_decode_kernel, FH=FH, scale=Dh ** -0.5, G=G),
        grid_spec=grid_spec,
        out_shape=jax.ShapeDtypeStruct((DB, R, Dh), F32),
        compiler_params=_params(("parallel", "arbitrary"), 56),
        name="fox_decode",
    )(pt, q2, kn2, vn2, *([cache_k2] * G), *([cache_v2] * G), *([suf] * G), *([tot] * G), ccol, crow)


def _convglu_kernel(u_ref, uh_ref, up_ref, c0_ref, cw_ref, cb_ref, o_ref):
    first = pl.program_id(1) == 0
    u = u_ref[...]
    p2 = jnp.where(first, c0_ref[0:1, :], uh_ref[SUBLANES - 2:SUBLANES - 1, :])
    p1 = jnp.where(first, c0_ref[1:2, :], uh_ref[SUBLANES - 1:SUBLANES, :])
    u1 = _shift_rows(u, [p1])
    u2 = _shift_rows(u, [p2, p1])
    c = cb_ref[...] + cw_ref[0:1, :] * u2 + cw_ref[1:2, :] * u1 + cw_ref[2:3, :] * u
    act = 0.5 * c * (1.0 + jnp.tanh(0.7978845608028654 * (c + 0.044715 * c * c * c)))
    o_ref[...] = (act * up_ref[...]).astype(o_ref.dtype)


def convglu(u, up, grp, conv0, conv_w, conv_b, out_rows):
    F = u.shape[1]
    n_seq, T, row0 = grp
    R = _pick_tile(T, 2064, SUBLANES)
    cb = _pick_tile(F, 256, LANES)
    nR, nC = T // R, F // cb
    assert row0 % R == 0 and conv_w.shape[0] == 3
    rb0, hb0 = row0 // R, row0 // SUBLANES
    main = pl.BlockSpec((R, cb), lambda s, i, c: (rb0 + s * nR + i, c))
    halo = pl.BlockSpec((SUBLANES, cb),
                        lambda s, i, c: (jnp.maximum(hb0 + (s * T + i * R) // SUBLANES - 1, 0), c))
    in_specs = [main, halo, main,
                pl.BlockSpec((None, 2, cb), lambda s, i, c: (s, 0, c)),
                pl.BlockSpec((3, cb), lambda s, i, c: (0, c)),
                pl.BlockSpec((1, cb), lambda s, i, c: (0, c))]
    return pl.pallas_call(
        _convglu_kernel,
        grid=(n_seq, nR, nC),
        in_specs=in_specs,
        out_specs=pl.BlockSpec((R, cb), lambda s, i, c: (s * nR + i, c)),
        out_shape=jax.ShapeDtypeStruct((out_rows, F), BF16 if T % 16 == 0 else F32),
        compiler_params=_params(("parallel", "parallel", "parallel"), 48),
        name="convglu",
    )(u, u, up, conv0, conv_w, conv_b.reshape(1, F))


def _proj_layout(D, FH, Dh, RH, RN, DL, AL, GL):
    FW, RW = FH * Dh, RH * RN
    lay = {"FH": FH, "Dh": Dh, "RW": RW, "FW": FW}
    lay["q_off"], lay["k_off"], lay["v_off"] = 0, FW, 2 * FW
    lay["r_off"], lay["rk_off"], lay["rv_off"] = 3 * FW, 3 * FW + RW, 3 * FW + 2 * RW
    lay["ga_off"] = 3 * FW + 3 * RW
    lay["gb_off"] = lay["ga_off"] + D
    TW = _round_up(GL + DL + AL + FH, LANES)
    lay["TW"] = TW
    lay["t_off"] = _round_up(lay["gb_off"] + D, TW)
    lay["gd"], lay["wd"], lay["ad"], lay["ff"] = 0, GL, GL + DL, GL + DL + AL
    lay["NPJ"] = _round_up(lay["t_off"] + TW, 512)
    return lay


def _permute_cols(w, lay, D, DL, AL, GL):
    FW, RW, FH = lay["FW"], lay["RW"], lay["FH"]
    fox_end = 3 * FW + FH
    rkv = w[..., fox_end:fox_end + 3 * RW]
    wd = w[..., fox_end + 3 * RW:fox_end + 3 * RW + DL]
    ad = w[..., fox_end + 3 * RW + DL:fox_end + 3 * RW + DL + AL]
    gd = w[..., fox_end + 3 * RW + DL + AL:fox_end + 3 * RW + DL + AL + GL]
    gates = w[..., fox_end + 3 * RW + DL + AL + GL:]
    ff = w[..., 3 * FW:fox_end]
    lead = w.shape[:-1]
    pad1 = jnp.zeros(lead + (lay["t_off"] - (lay["gb_off"] + D),), w.dtype)
    pad2 = jnp.zeros(lead + (lay["NPJ"] - lay["t_off"] - (GL + DL + AL + FH),), w.dtype)
    return jnp.concatenate([w[..., :3 * FW], rkv, gates, pad1, gd, wd, ad, ff, pad2], axis=-1)


def _pad_lora(w_up, off, TW):
    L, RW = w_up.shape
    return jnp.zeros((TW, RW), BF16).at[off:off + L].set(w_up.astype(BF16))


def kernel(x_prompt, x_sample, cache_k, cache_v, cache_logf, state_wkv, state_shift, state_conv, page_table,
           meta_tokens, norm_mix_pre, norm_mix_post, norm_ffn_pre, norm_ffn_post, w_in, fox_forget_bias,
           rwkv_mu, rwkv_w0, rwkv_w_up, rwkv_a0, rwkv_a_up, rwkv_g_up, rwkv_k_k, rwkv_k_a, rwkv_r_k,
           rwkv_gn_g, rwkv_gn_b, w_o_fox, w_o_rwkv, w_out, ffn_w_gate, ffn_w_up, ffn_conv_w, ffn_conv_b,
           ffn_w_down):
    B, SEQ, D = x_prompt.shape
    DB, TS, _ = x_sample.shape
    n_pool, DEPTH, PAGE, FH, Dh = cache_k.shape
    NM = meta_tokens.shape[0]
    T = NM + SEQ
    RH, RN = rwkv_r_k.shape[1:]
    DL, AL, GL = rwkv_w_up.shape[1], rwkv_a_up.shape[1], rwkv_g_up.shape[1]
    F = ffn_w_gate.shape[2]
    CW = ffn_conv_w.shape[1]
    FW, RW = FH * Dh, RH * RN
    RP = 3 * RW + DL + AL + GL
    assert RN == CHUNK and FW == RW and CW == 3 and T % 16 == 0
    MP = B * T
    M = MP + DB * TS
    lay = _proj_layout(D, FH, Dh, RH, RN, DL, AL, GL)
    TW = lay["TW"]
    grp_p = (B, T, 0)
    grp_s = (DB, TS, MP)

    meta = jnp.broadcast_to(meta_tokens[None].astype(x_prompt.dtype), (B, NM, D))
    x = jnp.concatenate([jnp.concatenate([meta, x_prompt], axis=1).reshape(MP, D),
                         x_sample.reshape(DB * TS, D)], axis=0)

    cache_k2 = cache_k.reshape(n_pool, DEPTH, PAGE * FH, Dh)
    cache_v2 = cache_v.reshape(n_pool, DEPTH, PAGE * FH, Dh)
    suf, tot = fox_page_sums(cache_logf.reshape(n_pool * DEPTH, PAGE * FH), FH)
    suf = suf.reshape(n_pool * DEPTH, 1, PAGE * FH)
    tot = tot.reshape(n_pool * DEPTH, 1, PAGE * FH)

    def split_rows(a):
        return a[:MP].reshape(B, T, -1), a[MP:].reshape(DB, TS, -1)

    def join_rows(big, small):
        return lax.dynamic_update_slice(big, small.astype(big.dtype), (MP, 0))

    def orig_rwkv_cols(p):
        t = p[..., lay["t_off"]:]
        return jnp.concatenate([p[..., lay["r_off"]:lay["r_off"] + 3 * RW], t[..., lay["wd"]:lay["wd"] + DL],
                                t[..., lay["ad"]:lay["ad"] + AL], t[..., lay["gd"]:lay["gd"] + GL]], axis=-1)

    def rwkv_cols_split(v):
        tail = jnp.zeros(v.shape[:-1] + (TW,), v.dtype)
        tail = tail.at[..., lay["gd"]:lay["gd"] + GL].set(v[..., 3 * RW + DL + AL:])
        tail = tail.at[..., lay["wd"]:lay["wd"] + DL].set(v[..., 3 * RW:3 * RW + DL])
        tail = tail.at[..., lay["ad"]:lay["ad"] + AL].set(v[..., 3 * RW + DL:3 * RW + DL + AL])
        return v[..., :RW], v[..., RW:2 * RW], v[..., 2 * RW:3 * RW], tail

    h = rmsnorm_bf16(x, norm_mix_pre[0])
    rows_p, rows_s = [], []
    for l in range(DEPTH):
        w_in_l = _permute_cols(w_in[l].astype(BF16), lay, D, DL, AL, GL)
        proj = matmul(h, w_in_l)

        ff_p, ff_s = split_rows(proj[:, lay["t_off"] + lay["ff"]:lay["t_off"] + lay["ff"] + FH])
        lf_p, c_p = fox_lf(ff_p, fox_forget_bias[l])
        lf_s, c_s = fox_lf(ff_s, fox_forget_bias[l])
        q2, kn2, vn2 = (proj[MP:, lay[n]:lay[n] + FW].reshape(DB, TS * FH, Dh) for n in ("q_off", "k_off", "v_off"))
        fox_s = fox_decode(q2, kn2, vn2, cache_k2, cache_v2, suf, tot, page_table, l, c_s, FH)
        fox = join_rows(fox_prompt(proj, lay, B, T, c_p, M), fox_s.reshape(DB * TS, FW))

        mu = [m.reshape(1, -1) for m in rwkv_cols_split(rwkv_mu[l])]
        lora = (_pad_lora(rwkv_w_up[l], lay["wd"], TW), _pad_lora(rwkv_a_up[l], lay["ad"], TW),
                _pad_lora(rwkv_g_up[l], lay["gd"], TW))
        vecs = [v.reshape(1, RW) for v in (rwkv_w0[l], rwkv_a0[l], rwkv_k_k[l], rwkv_k_a[l])]
        sh_p = [s[:, None, :] for s in rwkv_cols_split(jnp.zeros((B, RP), F32))]
        sh_s = [s[:, None, :] for s in rwkv_cols_split(state_shift[:, l])]
        prep_p = rwkv_prep(proj, lay, grp_p, sh_p, mu, *vecs, *lora)
        prep_s = rwkv_prep(proj, lay, grp_s, sh_s, mu, *vecs, *lora)
        s0_p = jnp.zeros((B, RW // LANES, LANES, CHUNK), F32)
        s0_s = jnp.swapaxes(state_wkv[:, l], 2, 3).reshape(DB, RW // LANES, LANES, CHUNK)
        scan_vecs = (rwkv_r_k[l].reshape(1, RW), rwkv_gn_g[l].reshape(1, RW), rwkv_gn_b[l].reshape(1, RW))
        rwkv_p, st_p = rwkv_scan(prep_p, B, T, s0_p, *scan_vecs, NM, M)
        rwkv_s, st_s = rwkv_scan(prep_s, DB, TS, s0_s, *scan_vecs, TS, DB * TS)
        rwkv = join_rows(rwkv_p, rwkv_s)

        def unpack_state(st):
            n = st.shape[0]
            st = st.reshape(n, RW // LANES, 2, CHUNK, 2, CHUNK)
            d = jnp.stack([st[:, :, 0, :, 0, :], st[:, :, 1, :, 1, :]], axis=2)
            return jnp.swapaxes(d.reshape(n, RH, CHUNK, CHUNK), 2, 3)

        mixed = gated_merge(fox, w_o_fox[l].astype(BF16), rwkv, w_o_rwkv[l].astype(BF16), proj,
                            lay["ga_off"], lay["gb_off"])
        y = matmul(mixed, w_out[l].astype(BF16))
        x, h2 = resnorm(y, x, norm_mix_post[l], norm_ffn_pre[l])

        u = matmul(h2, ffn_w_gate[l].astype(BF16))
        up = matmul(h2, ffn_w_up[l].astype(BF16))
        act = join_rows(convglu(u, up, grp_p, jnp.zeros((B, CW - 1, F), F32), ffn_conv_w[l], ffn_conv_b[l], M),
                        convglu(u, up, grp_s, state_conv[:, l], ffn_conv_w[l], ffn_conv_b[l], DB * TS))
        f = matmul(act, ffn_w_down[l].astype(BF16))
        x, h = resnorm(f, x, norm_ffn_post[l], norm_mix_pre[l + 1] if l + 1 < DEPTH else None)

        k_p, k_s = split_rows(proj[:, lay["k_off"]:lay["k_off"] + FW])
        v_p, v_s = split_rows(proj[:, lay["v_off"]:lay["v_off"] + FW])
        u_p, u_s = split_rows(u)
        sh_new_p = orig_rwkv_cols(proj[:MP].reshape(B, T, -1)[:, -1])
        sh_new_s = orig_rwkv_cols(proj[MP:].reshape(DB, TS, -1)[:, -1])
        rows_p.append((k_p.reshape(B, T, FH, Dh), v_p.reshape(B, T, FH, Dh), lf_p, unpack_state(st_p),
                       sh_new_p, u_p[:, T - (CW - 1):]))
        rows_s.append((k_s.reshape(DB, TS, FH, Dh), v_s.reshape(DB, TS, FH, Dh), lf_s, unpack_state(st_s),
                       sh_new_s, u_s[:, TS - (CW - 1):]))

    def stack(rows, i):
        return jnp.stack([r[i] for r in rows], axis=1)

    y_prompt = x[:MP].reshape(B, T, D)[:, NM:]
    y_sample = x[MP:].reshape(DB, TS, D)
    return (y_prompt, y_sample,
            *(stack(rows_p, i) for i in range(6)),
            *(stack(rows_s, i) for i in range(6)))
```

```python
import functools

import jax
import jax.numpy as jnp
from jax import lax
from jax.experimental import pallas as pl
from jax.experimental.pallas import tpu as pltpu

F32 = jnp.float32
BF16 = jnp.bfloat16
HIGHEST = lax.Precision.HIGHEST

NORM_EPS = 1e-6
GN_EPS = 64e-5
NEG_INF = -1e30

LANES = 128
SUBLANES = 8
CHUNK = 64
MIB = 2 ** 20


def _round_up(x, m):
    return (x + m - 1) // m * m


def _pick_tile(n, target, align):
    best = None
    for d in range(align, min(n, target) + 1, align):
        if n % d == 0:
            best = d
    return n if best is None else best


def _params(sem, vmem_mib):
    return pltpu.CompilerParams(dimension_semantics=sem, vmem_limit_bytes=int(vmem_mib * MIB))


def _rms(x, g):
    return x * lax.rsqrt(jnp.mean(x * x, axis=-1, keepdims=True) + NORM_EPS) * g


def _rmsnorm_kernel(x_ref, g_ref, o_ref):
    o_ref[...] = _rms(x_ref[...], g_ref[...]).astype(o_ref.dtype)


def rmsnorm_bf16(x, g):
    M, D = x.shape
    tr = _pick_tile(M, 512, 16)
    return pl.pallas_call(
        _rmsnorm_kernel,
        grid=(M // tr,),
        in_specs=[pl.BlockSpec((tr, D), lambda i: (i, 0)), pl.BlockSpec((1, D), lambda i: (0, 0))],
        out_specs=pl.BlockSpec((tr, D), lambda i: (i, 0)),
        out_shape=jax.ShapeDtypeStruct((M, D), BF16),
        compiler_params=_params(("parallel",), 48),
        name="rmsnorm",
    )(x, g.reshape(1, D))


def _resnorm_kernel(y_ref, x_ref, gp_ref, gn_ref, xo_ref, ho_ref):
    xn = x_ref[...] + _rms(y_ref[...], gp_ref[...])
    xo_ref[...] = xn
    ho_ref[...] = _rms(xn, gn_ref[...]).astype(ho_ref.dtype)


def _resnorm_last_kernel(y_ref, x_ref, gp_ref, xo_ref):
    xo_ref[...] = x_ref[...] + _rms(y_ref[...], gp_ref[...])


def resnorm(y, x, g_post, g_next):
    M, D = x.shape
    tr = _pick_tile(M, 512, 16)
    row = pl.BlockSpec((tr, D), lambda i: (i, 0))
    vec = pl.BlockSpec((1, D), lambda i: (0, 0))
    if g_next is None:
        return pl.pallas_call(
            _resnorm_last_kernel, grid=(M // tr,), in_specs=[row, row, vec], out_specs=row,
            out_shape=jax.ShapeDtypeStruct((M, D), F32),
            compiler_params=_params(("parallel",), 48), name="resnorm_last",
        )(y, x, g_post.reshape(1, D)), None
    return pl.pallas_call(
        _resnorm_kernel, grid=(M // tr,), in_specs=[row, row, vec, vec], out_specs=[row, row],
        out_shape=[jax.ShapeDtypeStruct((M, D), F32), jax.ShapeDtypeStruct((M, D), BF16)],
        compiler_params=_params(("parallel",), 56), name="resnorm",
    )(y, x, g_post.reshape(1, D), g_next.reshape(1, D))


def _mm_kernel(a_ref, w_ref, o_ref):
    o_ref[...] = jnp.dot(a_ref[...], w_ref[...], preferred_element_type=F32).astype(o_ref.dtype)


def _mm_acc_kernel(a_ref, w_ref, o_ref, acc_ref):
    k = pl.program_id(2)

    @pl.when(k == 0)
    def _():
        acc_ref[...] = jnp.zeros_like(acc_ref)

    acc_ref[...] += jnp.dot(a_ref[...], w_ref[...], preferred_element_type=F32)

    @pl.when(k == pl.num_programs(2) - 1)
    def _():
        o_ref[...] = acc_ref[...].astype(o_ref.dtype)


def matmul(a, w, layer=None, out_dtype=F32, tm_target=1024, tn_target=512, tk_target=6144):
    M, K = a.shape
    N = w.shape[-1]
    tm = _pick_tile(M, tm_target, 16)
    tn = _pick_tile(N, tn_target, LANES)
    tk = _pick_tile(K, tk_target, LANES)
    nk = K // tk
    osz = jnp.dtype(out_dtype).itemsize
    vmem = 2 * (tm * tk * 2 + tk * tn * 2 + tm * tn * osz) + tm * tn * 4
    vmem_mib = vmem / MIB + 8
    if layer is None:
        w_spec2 = pl.BlockSpec((K, tn), lambda i, j: (0, j))
        w_spec3 = pl.BlockSpec((tk, tn), lambda i, j, k: (k, j))
    else:
        w_spec2 = pl.BlockSpec((None, K, tn), lambda i, j: (layer, 0, j))
        w_spec3 = pl.BlockSpec((None, tk, tn), lambda i, j, k: (layer, k, j))
    if nk == 1:
        return pl.pallas_call(
            _mm_kernel,
            grid=(M // tm, N // tn),
            in_specs=[pl.BlockSpec((tm, K), lambda i, j: (i, 0)), w_spec2],
            out_specs=pl.BlockSpec((tm, tn), lambda i, j: (i, j)),
            out_shape=jax.ShapeDtypeStruct((M, N), out_dtype),
            compiler_params=_params(("parallel", "parallel"), vmem_mib),
            name="matmul",
        )(a, w)
    return pl.pallas_call(
        _mm_acc_kernel,
        grid=(M // tm, N // tn, nk),
        in_specs=[pl.BlockSpec((tm, tk), lambda i, j, k: (i, k)), w_spec3],
        out_specs=pl.BlockSpec((tm, tn), lambda i, j, k: (i, j)),
        out_shape=jax.ShapeDtypeStruct((M, N), out_dtype),
        scratch_shapes=[pltpu.VMEM((tm, tn), F32)],
        compiler_params=_params(("parallel", "parallel", "arbitrary"), vmem_mib),
        name="matmul_acc",
    )(a, w)


def _merge_kernel(a1_ref, w1_ref, a2_ref, w2_ref, ga_ref, gb_ref, o_ref):
    f = jnp.dot(a1_ref[...], w1_ref[...], preferred_element_type=F32)
    r = jnp.dot(a2_ref[...], w2_ref[...], preferred_element_type=F32)
    o_ref[...] = (jax.nn.sigmoid(ga_ref[...]) * f + jax.nn.sigmoid(gb_ref[...]) * r).astype(o_ref.dtype)


def gated_merge(fox, w_of, rwkv, w_or, layer, proj, ga_off, gb_off):
    M, K1 = fox.shape
    K2 = rwkv.shape[1]
    D = w_of.shape[2]
    tm = _pick_tile(M, 1024, 16)
    tn = _pick_tile(D, 512, LANES)
    assert ga_off % tn == 0 and gb_off % tn == 0
    ja, jb = ga_off // tn, gb_off // tn
    vmem = 2 * (tm * (K1 + K2) * 2 + (K1 + K2) * tn * 2 + 2 * tm * tn * 4 + tm * tn * 2) + 2 * tm * tn * 4
    return pl.pallas_call(
        _merge_kernel,
        grid=(M // tm, D // tn),
        in_specs=[
            pl.BlockSpec((tm, K1), lambda i, j: (i, 0)), pl.BlockSpec((None, K1, tn), lambda i, j: (layer, 0, j)),
            pl.BlockSpec((tm, K2), lambda i, j: (i, 0)), pl.BlockSpec((None, K2, tn), lambda i, j: (layer, 0, j)),
            pl.BlockSpec((tm, tn), lambda i, j: (i, ja + j)), pl.BlockSpec((tm, tn), lambda i, j: (i, jb + j)),
        ],
        out_specs=pl.BlockSpec((tm, tn), lambda i, j: (i, j)),
        out_shape=jax.ShapeDtypeStruct((M, D), BF16),
        compiler_params=_params(("parallel", "parallel"), vmem / MIB + 8),
        name="gated_merge",
    )(fox, w_of, rwkv, w_or, proj, proj)


def _shift_rows(x, fill_rows):
    n = len(fill_rows)
    out = pltpu.roll(x, n, axis=0)
    row = lax.broadcasted_iota(jnp.int32, x.shape, 0)
    for i, f in enumerate(fill_rows):
        out = jnp.where(row == i, f, out)
    return out


def _rwkv_prep_kernel(r_ref, k_ref, v_ref, t_ref, rh_ref, kh_ref, vh_ref, th_ref,
                      r0_ref, k0_ref, v0_ref, t0_ref, mur_ref, muk_ref, muv_ref, mut_ref,
                      w0_ref, a0_ref, kk_ref, ka_ref, wup_ref, aup_ref, gup_ref,
                      ro_ref, ld_ref, ko_ref, vo_ref, ao_ref, bo_ref, go_ref):
    first = pl.program_id(1) == 0

    def tokshift(x_ref, h_ref, s_ref, mu_ref):
        x = x_ref[...]
        prev0 = jnp.where(first, s_ref[...], h_ref[SUBLANES - 1:SUBLANES, :])
        prev = _shift_rows(x, [prev0])
        return x + (prev - x) * mu_ref[...]

    r = tokshift(r_ref, rh_ref, r0_ref, mur_ref)
    k = tokshift(k_ref, kh_ref, k0_ref, muk_ref)
    v = tokshift(v_ref, vh_ref, v0_ref, muv_ref)
    t = tokshift(t_ref, th_ref, t0_ref, mut_ref)

    wlin = w0_ref[...] + jnp.dot(jnp.tanh(t).astype(BF16), wup_ref[...], preferred_element_type=F32)
    w = -jax.nn.softplus(-wlin) - 0.5
    a = jax.nn.sigmoid(a0_ref[...] + jnp.dot(t.astype(BF16), aup_ref[...], preferred_element_type=F32))
    g = jnp.dot(jax.nn.sigmoid(t).astype(BF16), gup_ref[...], preferred_element_type=F32)

    kkp = k * kk_ref[...]
    sq = kkp * kkp
    cw = sq.shape[1]
    row = lax.broadcasted_iota(jnp.int32, (LANES, LANES), 0)
    col = lax.broadcasted_iota(jnp.int32, (LANES, LANES), 1)
    ones_bd = jnp.where(row // CHUNK == col // CHUNK, 1.0, 0.0).astype(F32)
    parts = [jnp.dot(sq[:, c:c + LANES], ones_bd, precision=HIGHEST, preferred_element_type=F32)
             for c in range(0, cw, LANES)]
    nrm = jnp.sqrt(jnp.concatenate(parts, axis=1) if len(parts) > 1 else parts[0])
    kk = kkp / jnp.maximum(nrm, 1e-12)

    ro_ref[...] = r.astype(ro_ref.dtype)
    ld_ref[...] = -jnp.exp(w)
    ko_ref[...] = (k * (1.0 + (a - 1.0) * ka_ref[...])).astype(ko_ref.dtype)
    vo_ref[...] = v.astype(vo_ref.dtype)
    ao_ref[...] = (-kk).astype(ao_ref.dtype)
    bo_ref[...] = (kk * a).astype(bo_ref.dtype)
    go_ref[...] = g.astype(go_ref.dtype)


def rwkv_prep(proj, lay, grp, shift0, mu, w0, a0, k_k, k_a, wup, aup, gup):
    RW, TW = lay["RW"], lay["TW"]
    n_seq, T, row0 = grp
    R = _pick_tile(T, 1024, 16)
    cb = _pick_tile(RW, 512, LANES)
    nR, nC = T // R, RW // cb
    rb0 = row0 // R
    hb0 = row0 // SUBLANES
    assert row0 % R == 0 and lay["r_off"] % cb == 0 and lay["t_off"] % TW == 0
    cr, ck, cv = (lay[n] // cb for n in ("r_off", "rk_off", "rv_off"))
    ct = lay["t_off"] // TW

    def main(c0):
        return pl.BlockSpec((R, cb), lambda s, i, c: (rb0 + s * nR + i, c0 + c))

    def halo(c0):
        return pl.BlockSpec((SUBLANES, cb),
                            lambda s, i, c: (jnp.maximum(hb0 + (s * T + i * R) // SUBLANES - 1, 0), c0 + c))

    tail = pl.BlockSpec((R, TW), lambda s, i, c: (rb0 + s * nR + i, ct))
    tail_h = pl.BlockSpec((SUBLANES, TW),
                          lambda s, i, c: (jnp.maximum(hb0 + (s * T + i * R) // SUBLANES - 1, 0), ct))
    s0 = pl.BlockSpec((None, 1, cb), lambda s, i, c: (s, 0, c))
    s0t = pl.BlockSpec((None, 1, TW), lambda s, i, c: (s, 0, 0))
    vec = pl.BlockSpec((1, cb), lambda s, i, c: (0, c))
    vect = pl.BlockSpec((1, TW), lambda s, i, c: (0, 0))
    lora = pl.BlockSpec((TW, cb), lambda s, i, c: (0, c))
    out = pl.BlockSpec((R, cb), lambda s, i, c: (s * nR + i, c))
    in_specs = [main(cr), main(ck), main(cv), tail, halo(cr), halo(ck), halo(cv), tail_h,
                s0, s0, s0, s0t, vec, vec, vec, vect, vec, vec, vec, vec, lora, lora, lora]
    args = [proj, proj, proj, proj, proj, proj, proj, proj, *shift0, *mu, w0, a0, k_k, k_a, wup, aup, gup]
    return pl.pallas_call(
        _rwkv_prep_kernel,
        grid=(n_seq, nR, nC),
        in_specs=in_specs,
        out_specs=[out] * 7,
        out_shape=[jax.ShapeDtypeStruct((n_seq * T, RW), F32 if (i == 1 or R % 16) else BF16) for i in range(7)],
        compiler_params=_params(("parallel", "parallel", "parallel"), 48),
        name="rwkv_prep",
    )(*args)


def _seg_sum(x, lo):
    s0 = jnp.sum(jnp.where(lo, x, 0.0), axis=1, keepdims=True)
    s1 = jnp.sum(jnp.where(lo, 0.0, x), axis=1, keepdims=True)
    return jnp.where(lo, s0, s1)


def _rwkv_scan_kernel(r_ref, ld_ref, k_ref, v_ref, a_ref, b_ref, g_ref, s0_ref, rk_ref, gng_ref, gnb_ref,
                      o_ref, so_ref, st_ref, *, n_chunks, n0, rows, n_dbl, pairs):
    C = CHUNK
    row = lax.broadcasted_iota(jnp.int32, (C, LANES), 0)
    lane = lax.broadcasted_iota(jnp.int32, (C, LANES), 1)
    lane_in = lane % C
    lo = lane < C
    m_strict = lane_in < row
    m_incl = lane_in <= row
    eye_pair = jnp.where(lane_in == row, 1.0, 0.0).astype(F32)
    row2 = lax.broadcasted_iota(jnp.int32, (2 * C, LANES), 0)
    lane2 = lax.broadcasted_iota(jnp.int32, (2 * C, LANES), 1)
    bdm = (row2 // C) == (lane2 // C)
    eye2 = row2 == lane2
    tr = lax.broadcasted_iota(jnp.int32, (C, C), 0)
    tc = lax.broadcasted_iota(jnp.int32, (C, C), 1)
    tril = jnp.where(tc <= tr, 1.0, 0.0).astype(F32)

    def bd(x):
        return jnp.where(bdm, jnp.concatenate([x, x], axis=0), 0.0)

    def mm(x, y):
        return jnp.dot(x.astype(BF16), y.astype(BF16), preferred_element_type=F32)

    def mm_tn(x, y):
        return mm(x.T, y)

    def mm_nt(x, y):
        return lax.dot_general(x.astype(BF16), y.astype(BF16), (((1,), (1,)), ((), ())),
                               preferred_element_type=F32)

    for p in range(pairs):
        st_ref[p] = jnp.where(bdm, jnp.concatenate([s0_ref[p], s0_ref[p]], axis=1), 0.0)

    def chunk(c, carry):
        if n_chunks == 1:
            start = 0
            nact = n0
        else:
            start = pl.multiple_of(jnp.where(c == 0, 0, n0 + (c - 1) * C), 16)
            nact = jnp.where(c == 0, n0, C)
        act = row < nact

        lss = [slice(p * LANES, (p + 1) * LANES) for p in range(pairs)]

        def each(f, *cols):
            return [f(*xs) for xs in zip(*cols)]

        def load(ref):
            def one(ls):
                if rows >= C:
                    return ref[pl.ds(start, C), ls].astype(F32)
                x = ref[:, ls].astype(F32)
                return jnp.concatenate([x, jnp.zeros((C - rows, LANES), F32)], axis=0)
            return [one(ls) for ls in lss]

        def masked(xs):
            return [jnp.where(act, x, 0.0) for x in xs]

        r_raw, k_raw, v_raw = load(r_ref), load(k_ref), load(v_ref)
        logd = masked(load(ld_ref))
        r, k, v = masked(r_raw), masked(k_raw), masked(v_raw)
        a, b = masked(load(a_ref)), masked(load(b_ref))

        lw = each(lambda x: jnp.dot(tril, x, precision=HIGHEST, preferred_element_type=F32), logd)
        lw_end = each(lambda x: x[C - 1:C, :], lw)
        at = each(lambda x, l, d: x * jnp.exp(l - d), a, lw, logd)
        rt = each(lambda x, l: x * jnp.exp(l), r, lw)
        dec_out = each(lambda l: jnp.exp(-l), lw)
        bt = each(jnp.multiply, b, dec_out)
        kt = each(jnp.multiply, k, dec_out)
        dec_tail = each(lambda e, l: jnp.exp(e - l), lw_end, lw)
        btl = each(jnp.multiply, b, dec_tail)
        ktl = each(jnp.multiply, k, dec_tail)

        big = each(lambda x, y, z, w: mm_nt(jnp.concatenate([x, y], axis=0),
                                            jnp.concatenate([bd(z), bd(w)], axis=0)), at, rt, bt, kt)
        a_ab = each(lambda x: jnp.where(m_strict, x[0:C, 0:LANES], 0.0), big)
        a_ak = each(lambda x: jnp.where(m_strict, x[0:C, LANES:2 * LANES], 0.0), big)
        a_rb = each(lambda x: jnp.where(m_incl, x[C:2 * C, 0:LANES], 0.0), big)
        a_rk = each(lambda x: jnp.where(m_incl, x[C:2 * C, LANES:2 * LANES], 0.0), big)

        pw = a_ab
        x = each(lambda m: eye_pair + m, a_ab)
        for _ in range(n_dbl):
            pw = each(lambda m: mm(m, bd(m)), pw)
            x = each(lambda xm, m: xm + mm(xm, bd(m)), x, pw)

        av_op = each(lambda m, n, vv: mm(jnp.concatenate([m, n], axis=0), bd(vv)), a_ak, a_rk, v)
        z = each(lambda xm, m, n: mm(xm, jnp.concatenate([bd(m), bd(n[0:C])], axis=1)), x, at, av_op)
        y = each(lambda m, zz: mm(m, jnp.concatenate([bd(zz[:, 0:LANES]), bd(zz[:, LANES:2 * LANES])], axis=1)),
                 a_rb, z)
        gmat = each(lambda m, yy: m + yy[:, 0:LANES], rt, y)
        hmat = each(lambda m, yy: m[C:2 * C] + yy[:, LANES:2 * LANES], av_op, y)
        m_mat = each(lambda e, m, zz: jnp.where(eye2, jnp.broadcast_to(jnp.exp(e), (2 * C, LANES)), 0.0)
                     + jnp.where(bdm, mm_tn(m, zz[:, 0:LANES]), 0.0), lw_end, btl, z)
        n_mat = each(lambda m, n, zz, vv: jnp.where(
            bdm, mm_tn(jnp.concatenate([m, n], axis=0),
                       jnp.concatenate([zz[:, LANES:2 * LANES], vv], axis=0)), 0.0), btl, ktl, z, v)

        st = [st_ref[p] for p in range(pairs)]
        o = each(lambda gm, s, hm: mm(gm, s) + hm, gmat, st, hmat)
        st_new = each(lambda mmat, s, nm: mm(mmat, s) + nm, m_mat, st, n_mat)
        for p in range(pairs):
            st_ref[p] = st_new[p]

        oc = each(lambda m: m - _seg_sum(m, lo) * (1.0 / C), o)
        var = each(lambda m: _seg_sum(m * m, lo) * (1.0 / C), oc)
        on = each(lambda m, vv, ls: m * lax.rsqrt(vv + GN_EPS) * gng_ref[:, ls] + gnb_ref[:, ls], oc, var, lss)
        bonus = each(lambda rr, kk, vv, ls: _seg_sum(rr * kk * rk_ref[:, ls], lo) * vv, r_raw, k_raw, v_raw, lss)
        res = each(lambda m, bo, g: ((m + bo) * g).astype(o_ref.dtype), on, bonus, load(g_ref))
        for ls, rs in zip(lss, res):
            if rows >= C:
                o_ref[pl.ds(start, C), ls] = rs
            else:
                o_ref[:, ls] = rs[0:rows]
        return carry

    if n_chunks == 1:
        chunk(0, 0)
    else:
        lax.fori_loop(0, n_chunks, chunk, 0)
    for p in range(pairs):
        so_ref[p] = st_ref[p]


def rwkv_scan(prep, n_seq, T, s0t, r_k, gn_g, gn_b, n0, out_rows):
    RW = prep[0].shape[1]
    pairs = next(p for p in (8, 4, 2, 1) if (RW // LANES) % p == 0)
    cb = pairs * LANES
    nP = RW // cb
    if T >= CHUNK:
        assert (T - n0) % CHUNK == 0 and n0 % 16 == 0
        n_chunks = 1 + (T - n0) // CHUNK
        nmax = CHUNK
    else:
        assert T == n0
        n_chunks = 1
        nmax = n0
    n_dbl = max(0, (nmax - 1).bit_length() - 1)
    seq = pl.BlockSpec((T, cb), lambda s, p: (s, p))
    seq_in = pl.BlockSpec((T, cb), lambda s, p: (s, p), pipeline_mode=pl.Buffered(1))
    vec = pl.BlockSpec((1, cb), lambda s, p: (0, p))
    st_in = pl.BlockSpec((None, pairs, LANES, CHUNK), lambda s, p: (s, p, 0, 0))
    st_out = pl.BlockSpec((None, pairs, LANES, LANES), lambda s, p: (s, p, 0, 0))
    out_dtype = BF16 if T % 16 == 0 else F32
    return pl.pallas_call(
        functools.partial(_rwkv_scan_kernel, n_chunks=n_chunks, n0=n0, rows=T, n_dbl=n_dbl, pairs=pairs),
        grid=(n_seq, nP),
        in_specs=[seq_in] * 7 + [st_in, vec, vec, vec],
        out_specs=[seq, st_out],
        out_shape=[jax.ShapeDtypeStruct((out_rows, RW), out_dtype),
                   jax.ShapeDtypeStruct((n_seq, RW // LANES, LANES, LANES), F32)],
        scratch_shapes=[pltpu.VMEM((pairs, LANES, LANES), F32)],
        compiler_params=_params(("parallel", "parallel"), 56),
        name="rwkv_scan",
    )(*prep, s0t, r_k, gn_g, gn_b)


def _fox_lf_kernel(ff_ref, bias_ref, lf_ref, c_ref, *, tb):
    T = ff_ref.shape[0]
    z = ff_ref[...] + bias_ref[...]
    lf = jnp.minimum(z, 0.0) - jnp.log1p(jnp.exp(-jnp.abs(z)))
    lf_ref[...] = lf
    for r0 in range(0, T, tb):
        row = lax.broadcasted_iota(jnp.int32, (tb, T), 0) + r0
        col = lax.broadcasted_iota(jnp.int32, (tb, T), 1)
        tri = jnp.where(col <= row, 1.0, 0.0).astype(F32)
        c_ref[r0:r0 + tb, :] = jnp.dot(tri, lf, precision=HIGHEST, preferred_element_type=F32)


def fox_lf(ff, bias):
    n_seq, T, H = ff.shape
    tb = _pick_tile(T, 512, SUBLANES)
    blk = pl.BlockSpec((None, T, H), lambda s: (s, 0, 0))
    return pl.pallas_call(
        functools.partial(_fox_lf_kernel, tb=tb),
        grid=(n_seq,),
        in_specs=[blk, pl.BlockSpec((1, H), lambda s: (0, 0))],
        out_specs=[blk, blk],
        out_shape=[jax.ShapeDtypeStruct((n_seq, T, H), F32)] * 2,
        compiler_params=_params(("parallel",), 48),
        name="fox_lf",
    )(ff, bias.reshape(1, H))


def _fox_prompt_kernel(q_ref, k_ref, v_ref, c_ref, ct_ref, o_ref, *, tq, scale):
    T, Dh = q_ref.shape
    nb = T // tq
    H = c_ref.shape[1]
    h = pl.program_id(1)
    hl = lax.broadcasted_iota(jnp.int32, (T, H), 1)
    ccol = jnp.sum(jnp.where(hl == h, c_ref[...], 0.0), axis=1, keepdims=True)
    crow = ct_ref[h]

    for qi in range(nb):
        n_keys = (qi + 1) * tq
        q = q_ref[qi * tq:(qi + 1) * tq, :].astype(BF16)
        kb = k_ref[0:n_keys, :].astype(BF16)
        s = lax.dot_general(q, kb, (((1,), (1,)), ((), ())), preferred_element_type=F32) * scale
        s = s + (ccol[qi * tq:(qi + 1) * tq, :] - crow[:, 0:n_keys])
        ri = lax.broadcasted_iota(jnp.int32, (tq, n_keys), 0) + qi * tq
        ci = lax.broadcasted_iota(jnp.int32, (tq, n_keys), 1)
        s = jnp.where(ci <= ri, s, NEG_INF)
        p = jnp.exp(s - jnp.max(s, axis=1, keepdims=True))
        l = jnp.sum(p, axis=1, keepdims=True)
        acc = jnp.dot(p.astype(BF16), v_ref[0:n_keys, :].astype(BF16), preferred_element_type=F32)
        o_ref[qi * tq:(qi + 1) * tq, :] = (acc / l).astype(o_ref.dtype)


def fox_prompt(proj, lay, B, T, c, out_rows):
    FH, Dh = lay["FH"], lay["Dh"]
    tq = _pick_tile(T, 512, SUBLANES)
    ct = jnp.swapaxes(c, 1, 2)[:, :, None, :]
    assert Dh % LANES == 0 and lay["q_off"] % Dh == 0
    cq, ck, cv = (lay[n] // Dh for n in ("q_off", "k_off", "v_off"))
    return pl.pallas_call(
        functools.partial(_fox_prompt_kernel, tq=tq, scale=Dh ** -0.5),
        grid=(B, FH),
        in_specs=[
            pl.BlockSpec((T, Dh), lambda b, h: (b, cq + h)),
            pl.BlockSpec((T, Dh), lambda b, h: (b, ck + h)),
            pl.BlockSpec((T, Dh), lambda b, h: (b, cv + h)),
            pl.BlockSpec((None, T, FH), lambda b, h: (b, 0, 0)),
            pl.BlockSpec((None, FH, 1, T), lambda b, h: (b, 0, 0, 0)),
        ],
        out_specs=pl.BlockSpec((T, Dh), lambda b, h: (b, h)),
        out_shape=jax.ShapeDtypeStruct((out_rows, FH * Dh), BF16),
        compiler_params=_params(("parallel", "parallel"), 48),
        name="fox_prompt",
    )(proj, proj, proj, c, ct)


def _page_sums_kernel(lf_ref, suf_ref, tot_ref, *, FH):
    x = lf_ref[...]
    hi = x.astype(BF16)
    lo = (x - hi.astype(F32)).astype(BF16)
    lo2 = (x - hi.astype(F32) - lo.astype(F32)).astype(BF16)
    nv = x.shape[1] // LANES
    r = lax.broadcasted_iota(jnp.int32, (LANES, 2 * LANES), 0)
    c = lax.broadcasted_iota(jnp.int32, (LANES, 2 * LANES), 1)
    same_head = (r % FH) == (c % FH)
    later = (r // FH) > ((c % LANES) // FH)
    sel = jnp.where(same_head & (later | (c >= LANES)), 1.0, 0.0).astype(BF16)
    carry = jnp.zeros((x.shape[0], LANES), F32)
    for v in reversed(range(nv)):
        ls = slice(v * LANES, (v + 1) * LANES)
        y = (jnp.dot(hi[:, ls], sel, preferred_element_type=F32)
             + jnp.dot(lo[:, ls], sel, preferred_element_type=F32)
             + jnp.dot(lo2[:, ls], sel, preferred_element_type=F32))
        suf_ref[:, ls] = y[:, 0:LANES] + carry
        carry = carry + y[:, LANES:2 * LANES]
    for v in range(nv):
        tot_ref[:, v * LANES:(v + 1) * LANES] = carry


def fox_page_sums(lf_rows, FH):
    n, W = lf_rows.shape
    assert LANES % FH == 0 and W % LANES == 0
    tr = _pick_tile(n, 256, SUBLANES)
    blk = pl.BlockSpec((tr, W), lambda i: (i, 0))
    return pl.pallas_call(
        functools.partial(_page_sums_kernel, FH=FH),
        grid=(n // tr,),
        in_specs=[blk],
        out_specs=[blk, blk],
        out_shape=[jax.ShapeDtypeStruct((n, W), F32)] * 2,
        compiler_params=_params(("parallel",), 48),
        name="fox_page_sums",
    )(lf_rows)


def _fox_decode_kernel(pt_ref, q_ref, kn_ref, vn_ref, *refs, FH, scale, G):
    kp_refs, vp_refs, suf_refs, tot_refs = (refs[i * G:(i + 1) * G] for i in range(4))
    ccol_ref, crow_ref, o_ref, m_ref, l_ref, acc_ref, r_ref, mask_ref = refs[4 * G:]
    j = pl.program_id(1)
    nj = pl.num_programs(1)
    R = q_ref.shape[0]

    @pl.when(j == 0)
    def _():
        m_ref[...] = jnp.full_like(m_ref, NEG_INF)
        l_ref[...] = jnp.zeros_like(l_ref)
        acc_ref[...] = jnp.zeros_like(acc_ref)
        r_ref[...] = jnp.zeros_like(r_ref)
        rr = lax.broadcasted_iota(jnp.int32, mask_ref.shape, 0)
        cc = lax.broadcasted_iota(jnp.int32, mask_ref.shape, 1)
        mask_ref[...] = jnp.where((rr % FH) == (cc % FH), 0.0, NEG_INF)

    q = q_ref[...].astype(BF16)
    ccol = ccol_ref[...]

    def update(ss, vbs):
        m_old = m_ref[...]
        m_new = m_old
        for s in ss:
            m_new = jnp.maximum(m_new, jnp.max(s, axis=1, keepdims=True))
        alpha = jnp.exp(m_old - m_new)
        ps = [jnp.exp(s - m_new) for s in ss]
        l_new = alpha * l_ref[...]
        acc = alpha * acc_ref[...]
        for p, vb in zip(ps, vbs):
            l_new = l_new + jnp.sum(p, axis=1, keepdims=True)
            acc = acc + jnp.dot(p.astype(BF16), vb, preferred_element_type=F32)
        l_ref[...] = l_new
        acc_ref[...] = acc
        m_ref[...] = m_new

    def scores(k_ref):
        return lax.dot_general(q, k_ref[...].astype(BF16), (((1,), (1,)), ((), ())),
                               preferred_element_type=F32) * scale

    ss = [scores(k_ref) for k_ref in kp_refs]
    later = r_ref[...]
    for g in range(G):
        ss[g] = ss[g] + (mask_ref[...] + (suf_refs[g][...] + later)) + ccol
        later = later + tot_refs[g][...]
    update(ss, [v_ref[...].astype(BF16) for v_ref in vp_refs])
    r_ref[...] = later

    @pl.when(j == nj - 1)
    def _():
        ri = lax.broadcasted_iota(jnp.int32, (R, R), 0)
        ci = lax.broadcasted_iota(jnp.int32, (R, R), 1)
        ok = ((ri % FH) == (ci % FH)) & ((ci // FH) <= (ri // FH))
        sn = jnp.where(ok, scores(kn_ref) + (ccol - crow_ref[...]), NEG_INF)
        update([sn], [vn_ref[...].astype(BF16)])
        o_ref[...] = (acc_ref[...] / l_ref[...]).astype(o_ref.dtype)


def fox_decode(q2, kn2, vn2, cache_k2, cache_v2, suf, tot, page_table, layer, c_new, FH):
    DB, R, Dh = q2.shape
    NP = page_table.shape[1]
    DEPTH, W = cache_k2.shape[1], cache_k2.shape[2]
    ccol = c_new.reshape(DB, R, 1)
    crow = c_new.reshape(DB, 1, R)
    pt = page_table.reshape(-1)
    G = next(g for g in (4, 2, 1) if NP % g == 0)
    new_spec = pl.BlockSpec((None, R, Dh), lambda b, j, pt: (b, 0, 0))

    def page(b, j, pt, g):
        return pt[b * NP + NP - 1 - (j * G + g)]

    page_specs = [pl.BlockSpec((None, None, W, Dh), lambda b, j, pt, g=g: (page(b, j, pt, g), layer, 0, 0))
                  for g in range(G)]
    row_specs = [pl.BlockSpec((None, 1, W), lambda b, j, pt, g=g: (page(b, j, pt, g) * DEPTH + layer, 0, 0))
                 for g in range(G)]
    grid_spec = pltpu.PrefetchScalarGridSpec(
        num_scalar_prefetch=1,
        grid=(DB, NP // G),
        in_specs=[new_spec, new_spec, new_spec, *page_specs, *page_specs, *row_specs, *row_specs,
                  pl.BlockSpec((None, R, 1), lambda b, j, pt: (b, 0, 0)),
                  pl.BlockSpec((None, 1, R), lambda b, j, pt: (b, 0, 0))],
        out_specs=pl.BlockSpec((None, R, Dh), lambda b, j, pt: (b, 0, 0)),
        scratch_shapes=[pltpu.VMEM((R, 1), F32), pltpu.VMEM((R, 1), F32), pltpu.VMEM((R, Dh), F32),
                        pltpu.VMEM((1, W), F32), pltpu.VMEM((R, W), F32)],
    )
    return pl.pallas_call(
        functools.partial(_fox_decode_kernel, FH=FH, scale=Dh ** -0.5, G=G),
        grid_spec=grid_spec,
        out_shape=jax.ShapeDtypeStruct((DB, R, Dh), F32),
        compiler_params=_params(("parallel", "arbitrary"), 56),
        name="fox_decode",
    )(pt, q2, kn2, vn2, *([cache_k2] * G), *([cache_v2] * G), *([suf] * G), *([tot] * G), ccol, crow)


def _glu(u, up, p2, p1, cw_ref, cb_ref):
    u1 = _shift_rows(u, [p1])
    u2 = _shift_rows(u, [p2, p1])
    c = cb_ref[...] + cw_ref[0:1, :] * u2 + cw_ref[1:2, :] * u1 + cw_ref[2:3, :] * u
    act = 0.5 * c * (1.0 + jnp.tanh(0.7978845608028654 * (c + 0.044715 * c * c * c)))
    return act * up


def _convglu_kernel(u_ref, uh_ref, up_ref, c0_ref, cw_ref, cb_ref, o_ref):
    first = pl.program_id(1) == 0
    p2 = jnp.where(first, c0_ref[0:1, :], uh_ref[SUBLANES - 2:SUBLANES - 1, :])
    p1 = jnp.where(first, c0_ref[1:2, :], uh_ref[SUBLANES - 1:SUBLANES, :])
    o_ref[...] = _glu(u_ref[...], up_ref[...], p2, p1, cw_ref, cb_ref).astype(o_ref.dtype)


def _ffn_glu_kernel(a_ref, ah_ref, wg_ref, wu_ref, c0_ref, cw_ref, cb_ref, o_ref, ul_ref, *, tiles_per_seq):
    first = (pl.program_id(0) % tiles_per_seq) == 0
    a = a_ref[...]
    wg = wg_ref[...]
    u = jnp.dot(a, wg, preferred_element_type=F32)
    uh = jnp.dot(ah_ref[...], wg, preferred_element_type=F32)
    up = jnp.dot(a, wu_ref[...], preferred_element_type=F32)
    nh, tm = uh.shape[0], u.shape[0]
    p2 = jnp.where(first, c0_ref[0:1, :], uh[nh - 2:nh - 1, :])
    p1 = jnp.where(first, c0_ref[1:2, :], uh[nh - 1:nh, :])
    o_ref[...] = _glu(u, up, p2, p1, cw_ref, cb_ref).astype(o_ref.dtype)
    ul_ref[...] = u[tm - SUBLANES:tm, :]


def ffn_glu(h, w_gate, w_up, layer, n_seq, T, conv0, conv_w, conv_b, out_rows):
    K = h.shape[1]
    F = w_gate.shape[2]
    tm = _pick_tile(T, 1024, 16)
    tn = _pick_tile(F, 256, LANES)
    HB = 16
    tps = T // tm
    n_tiles = n_seq * tps
    w_spec = pl.BlockSpec((None, K, tn), lambda i, j: (layer, 0, j))
    vmem = 2 * (tm * K * 2 + HB * K * 2 + 2 * K * tn * 2 + tm * tn * 2) + 4 * tm * tn * 4
    return pl.pallas_call(
        functools.partial(_ffn_glu_kernel, tiles_per_seq=tps),
        grid=(n_tiles, F // tn),
        in_specs=[pl.BlockSpec((tm, K), lambda i, j: (i, 0)),
                  pl.BlockSpec((HB, K), lambda i, j: (jnp.maximum(i * (tm // HB) - 1, 0), 0)),
                  w_spec, w_spec,
                  pl.BlockSpec((None, 2, tn), lambda i, j: (i // tps, 0, j)),
                  pl.BlockSpec((3, tn), lambda i, j: (0, j)),
                  pl.BlockSpec((1, tn), lambda i, j: (0, j))],
        out_specs=[pl.BlockSpec((tm, tn), lambda i, j: (i, j)),
                   pl.BlockSpec((None, SUBLANES, tn), lambda i, j: (i, 0, j))],
        out_shape=[jax.ShapeDtypeStruct((out_rows, F), BF16),
                   jax.ShapeDtypeStruct((n_tiles, SUBLANES, F), F32)],
        compiler_params=_params(("parallel", "parallel"), vmem / MIB + 8),
        name="ffn_glu",
    )(h, h, w_gate, w_up, conv0, conv_w, conv_b.reshape(1, F))


def convglu(u, up, grp, conv0, conv_w, conv_b, out_rows):
    F = u.shape[1]
    n_seq, T, row0 = grp
    R = _pick_tile(T, 2064, SUBLANES)
    cb = _pick_tile(F, 256, LANES)
    nR, nC = T // R, F // cb
    assert row0 % R == 0 and conv_w.shape[0] == 3
    rb0, hb0 = row0 // R, row0 // SUBLANES
    main = pl.BlockSpec((R, cb), lambda s, i, c: (rb0 + s * nR + i, c))
    halo = pl.BlockSpec((SUBLANES, cb),
                        lambda s, i, c: (jnp.maximum(hb0 + (s * T + i * R) // SUBLANES - 1, 0), c))
    in_specs = [main, halo, main,
                pl.BlockSpec((None, 2, cb), lambda s, i, c: (s, 0, c)),
                pl.BlockSpec((3, cb), lambda s, i, c: (0, c)),
                pl.BlockSpec((1, cb), lambda s, i, c: (0, c))]
    return pl.pallas_call(
        _convglu_kernel,
        grid=(n_seq, nR, nC),
        in_specs=in_specs,
        out_specs=pl.BlockSpec((R, cb), lambda s, i, c: (s * nR + i, c)),
        out_shape=jax.ShapeDtypeStruct((out_rows, F), BF16 if T % 16 == 0 else F32),
        compiler_params=_params(("parallel", "parallel", "parallel"), 48),
        name="convglu",
    )(u, u, up, conv0, conv_w, conv_b.reshape(1, F))


def _proj_layout(D, FH, Dh, RH, RN, DL, AL, GL):
    FW, RW = FH * Dh, RH * RN
    lay = {"FH": FH, "Dh": Dh, "RW": RW, "FW": FW}
    lay["q_off"], lay["k_off"], lay["v_off"] = 0, FW, 2 * FW
    lay["r_off"], lay["rk_off"], lay["rv_off"] = 3 * FW, 3 * FW + RW, 3 * FW + 2 * RW
    lay["ga_off"] = 3 * FW + 3 * RW
    lay["gb_off"] = lay["ga_off"] + D
    TW = _round_up(GL + DL + AL + FH, LANES)
    lay["TW"] = TW
    lay["t_off"] = _round_up(lay["gb_off"] + D, TW)
    lay["gd"], lay["wd"], lay["ad"], lay["ff"] = 0, GL, GL + DL, GL + DL + AL
    lay["NPJ"] = _round_up(lay["t_off"] + TW, 512)
    return lay


def _permute_cols(w, lay, D, DL, AL, GL):
    FW, RW, FH = lay["FW"], lay["RW"], lay["FH"]
    fox_end = 3 * FW + FH
    rkv = w[..., fox_end:fox_end + 3 * RW]
    wd = w[..., fox_end + 3 * RW:fox_end + 3 * RW + DL]
    ad = w[..., fox_end + 3 * RW + DL:fox_end + 3 * RW + DL + AL]
    gd = w[..., fox_end + 3 * RW + DL + AL:fox_end + 3 * RW + DL + AL + GL]
    gates = w[..., fox_end + 3 * RW + DL + AL + GL:]
    ff = w[..., 3 * FW:fox_end]
    lead = w.shape[:-1]
    pad1 = jnp.zeros(lead + (lay["t_off"] - (lay["gb_off"] + D),), w.dtype)
    pad2 = jnp.zeros(lead + (lay["NPJ"] - lay["t_off"] - (GL + DL + AL + FH),), w.dtype)
    return jnp.concatenate([w[..., :3 * FW], rkv, gates, pad1, gd, wd, ad, ff, pad2], axis=-1)


def _pad_lora(w_up, off, TW):
    L, RW = w_up.shape
    return jnp.zeros((TW, RW), BF16).at[off:off + L].set(w_up.astype(BF16))


def kernel(x_prompt, x_sample, cache_k, cache_v, cache_logf, state_wkv, state_shift, state_conv, page_table,
           meta_tokens, norm_mix_pre, norm_mix_post, norm_ffn_pre, norm_ffn_post, w_in, fox_forget_bias,
           rwkv_mu, rwkv_w0, rwkv_w_up, rwkv_a0, rwkv_a_up, rwkv_g_up, rwkv_k_k, rwkv_k_a, rwkv_r_k,
           rwkv_gn_g, rwkv_gn_b, w_o_fox, w_o_rwkv, w_out, ffn_w_gate, ffn_w_up, ffn_conv_w, ffn_conv_b,
           ffn_w_down):
    B, SEQ, D = x_prompt.shape
    DB, TS, _ = x_sample.shape
    n_pool, DEPTH, PAGE, FH, Dh = cache_k.shape
    NM = meta_tokens.shape[0]
    T = NM + SEQ
    RH, RN = rwkv_r_k.shape[1:]
    DL, AL, GL = rwkv_w_up.shape[1], rwkv_a_up.shape[1], rwkv_g_up.shape[1]
    F = ffn_w_gate.shape[2]
    CW = ffn_conv_w.shape[1]
    FW, RW = FH * Dh, RH * RN
    RP = 3 * RW + DL + AL + GL
    assert RN == CHUNK and FW == RW and CW == 3 and T % 16 == 0
    MP = B * T
    M = MP + DB * TS
    lay = _proj_layout(D, FH, Dh, RH, RN, DL, AL, GL)
    TW = lay["TW"]
    grp_p = (B, T, 0)
    grp_s = (DB, TS, MP)

    meta = jnp.broadcast_to(meta_tokens[None].astype(x_prompt.dtype), (B, NM, D))
    x = jnp.concatenate([jnp.concatenate([meta, x_prompt], axis=1).reshape(MP, D),
                         x_sample.reshape(DB * TS, D)], axis=0)

    cache_k2 = cache_k.reshape(n_pool, DEPTH, PAGE * FH, Dh)
    cache_v2 = cache_v.reshape(n_pool, DEPTH, PAGE * FH, Dh)
    suf, tot = fox_page_sums(cache_logf.reshape(n_pool * DEPTH, PAGE * FH), FH)
    suf = suf.reshape(n_pool * DEPTH, 1, PAGE * FH)
    tot = tot.reshape(n_pool * DEPTH, 1, PAGE * FH)

    def split_rows(a):
        return a[:MP].reshape(B, T, -1), a[MP:].reshape(DB, TS, -1)

    def join_rows(big, small):
        return lax.dynamic_update_slice(big, small.astype(big.dtype), (MP, 0))

    def orig_rwkv_cols(p):
        t = p[..., lay["t_off"]:]
        return jnp.concatenate([p[..., lay["r_off"]:lay["r_off"] + 3 * RW], t[..., lay["wd"]:lay["wd"] + DL],
                                t[..., lay["ad"]:lay["ad"] + AL], t[..., lay["gd"]:lay["gd"] + GL]], axis=-1)

    def rwkv_cols_split(v):
        tail = jnp.zeros(v.shape[:-1] + (TW,), v.dtype)
        tail = tail.at[..., lay["gd"]:lay["gd"] + GL].set(v[..., 3 * RW + DL + AL:])
        tail = tail.at[..., lay["wd"]:lay["wd"] + DL].set(v[..., 3 * RW:3 * RW + DL])
        tail = tail.at[..., lay["ad"]:lay["ad"] + AL].set(v[..., 3 * RW + DL:3 * RW + DL + AL])
        return v[..., :RW], v[..., RW:2 * RW], v[..., 2 * RW:3 * RW], tail

    w_of, w_or, w_o = w_o_fox.astype(BF16), w_o_rwkv.astype(BF16), w_out.astype(BF16)
    w_gate, w_up, w_down = ffn_w_gate.astype(BF16), ffn_w_up.astype(BF16), ffn_w_down.astype(BF16)

    h = rmsnorm_bf16(x, norm_mix_pre[0])
    rows_p, rows_s = [], []
    for l in range(DEPTH):
        w_in_l = _permute_cols(w_in[l].astype(BF16), lay, D, DL, AL, GL)
        proj = matmul(h, w_in_l)

        ff_p, ff_s = split_rows(proj[:, lay["t_off"] + lay["ff"]:lay["t_off"] + lay["ff"] + FH])
        lf_p, c_p = fox_lf(ff_p, fox_forget_bias[l])
        lf_s, c_s = fox_lf(ff_s, fox_forget_bias[l])
        q2, kn2, vn2 = (proj[MP:, lay[n]:lay[n] + FW].reshape(DB, TS * FH, Dh) for n in ("q_off", "k_off", "v_off"))
        fox_s = fox_decode(q2, kn2, vn2, cache_k2, cache_v2, suf, tot, page_table, l, c_s, FH)
        fox = join_rows(fox_prompt(proj, lay, B, T, c_p, M), fox_s.reshape(DB * TS, FW))

        mu = [m.reshape(1, -1) for m in rwkv_cols_split(rwkv_mu[l])]
        lora = (_pad_lora(rwkv_w_up[l], lay["wd"], TW), _pad_lora(rwkv_a_up[l], lay["ad"], TW),
                _pad_lora(rwkv_g_up[l], lay["gd"], TW))
        vecs = [v.reshape(1, RW) for v in (rwkv_w0[l], rwkv_a0[l], rwkv_k_k[l], rwkv_k_a[l])]
        sh_p = [s[:, None, :] for s in rwkv_cols_split(jnp.zeros((B, RP), F32))]
        sh_s = [s[:, None, :] for s in rwkv_cols_split(state_shift[:, l])]
        prep_p = rwkv_prep(proj, lay, grp_p, sh_p, mu, *vecs, *lora)
        prep_s = rwkv_prep(proj, lay, grp_s, sh_s, mu, *vecs, *lora)
        s0_p = jnp.zeros((B, RW // LANES, LANES, CHUNK), F32)
        s0_s = jnp.swapaxes(state_wkv[:, l], 2, 3).reshape(DB, RW // LANES, LANES, CHUNK)
        scan_vecs = (rwkv_r_k[l].reshape(1, RW), rwkv_gn_g[l].reshape(1, RW), rwkv_gn_b[l].reshape(1, RW))
        rwkv_p, st_p = rwkv_scan(prep_p, B, T, s0_p, *scan_vecs, NM, M)
        rwkv_s, st_s = rwkv_scan(prep_s, DB, TS, s0_s, *scan_vecs, TS, DB * TS)
        rwkv = join_rows(rwkv_p, rwkv_s)

        def unpack_state(st):
            n = st.shape[0]
            st = st.reshape(n, RW // LANES, 2, CHUNK, 2, CHUNK)
            d = jnp.stack([st[:, :, 0, :, 0, :], st[:, :, 1, :, 1, :]], axis=2)
            return jnp.swapaxes(d.reshape(n, RH, CHUNK, CHUNK), 2, 3)

        mixed = gated_merge(fox, w_of, rwkv, w_or, l, proj, lay["ga_off"], lay["gb_off"])
        y = matmul(mixed, w_o, l)
        x, h2 = resnorm(y, x, norm_mix_post[l], norm_ffn_pre[l])

        act_p, u_last = ffn_glu(h2, w_gate, w_up, l, B, T, jnp.zeros((B, CW - 1, F), F32),
                                ffn_conv_w[l], ffn_conv_b[l], M)
        h2_s = h2[MP:]
        u_s = matmul(h2_s, w_gate, l)
        up_s = matmul(h2_s, w_up, l)
        act_s = convglu(u_s, up_s, (DB, TS, 0), state_conv[:, l], ffn_conv_w[l], ffn_conv_b[l], DB * TS)
        f = matmul(join_rows(act_p, act_s), w_down, l)
        x, h = resnorm(f, x, norm_ffn_post[l], norm_mix_pre[l + 1] if l + 1 < DEPTH else None)

        k_p, k_s = split_rows(proj[:, lay["k_off"]:lay["k_off"] + FW])
        v_p, v_s = split_rows(proj[:, lay["v_off"]:lay["v_off"] + FW])
        tps = u_last.shape[0] // B
        conv_p = u_last[tps - 1::tps, SUBLANES - (CW - 1):]
        conv_s = u_s.reshape(DB, TS, F)[:, TS - (CW - 1):]
        sh_new_p = orig_rwkv_cols(proj[T - 1:MP:T])
        sh_new_s = orig_rwkv_cols(proj[MP + TS - 1::TS])
        rows_p.append((k_p.reshape(B, T, FH, Dh), v_p.reshape(B, T, FH, Dh), lf_p, unpack_state(st_p),
                       sh_new_p, conv_p))
        rows_s.append((k_s.reshape(DB, TS, FH, Dh), v_s.reshape(DB, TS, FH, Dh), lf_s, unpack_state(st_s),
                       sh_new_s, conv_s))

    def stack(rows, i):
        return jnp.stack([r[i] for r in rows], axis=1)

    y_prompt = x[:MP].reshape(B, T, D)[:, NM:]
    y_sample = x[MP:].reshape(DB, TS, D)
    return (y_prompt, y_sample,
            *(stack(rows_p, i) for i in range(6)),
            *(stack(rows_s, i) for i in range(6)))
```

```python
import functools

import jax
import jax.numpy as jnp
from jax import lax
from jax.experimental import pallas as pl
from jax.experimental.pallas import tpu as pltpu

F32 = jnp.float32
BF16 = jnp.bfloat16
HIGHEST = lax.Precision.HIGHEST

NORM_EPS = 1e-6
GN_EPS = 64e-5
NEG_INF = -1e30

LANES = 128
SUBLANES = 8
CHUNK = 64
MIB = 2 ** 20


def _round_up(x, m):
    return (x + m - 1) // m * m


def _pick_tile(n, target, align):
    best = None
    for d in range(align, min(n, target) + 1, align):
        if n % d == 0:
            best = d
    return n if best is None else best


def _params(sem, vmem_mib):
    return pltpu.CompilerParams(dimension_semantics=sem, vmem_limit_bytes=int(vmem_mib * MIB))


def _rms(x, g):
    return x * lax.rsqrt(jnp.mean(x * x, axis=-1, keepdims=True) + NORM_EPS) * g


def _rmsnorm_kernel(x_ref, g_ref, o_ref):
    o_ref[...] = _rms(x_ref[...], g_ref[...]).astype(o_ref.dtype)


def rmsnorm_bf16(x, g):
    M, D = x.shape
    tr = _pick_tile(M, 512, 16)
    return pl.pallas_call(
        _rmsnorm_kernel,
        grid=(M // tr,),
        in_specs=[pl.BlockSpec((tr, D), lambda i: (i, 0)), pl.BlockSpec((1, D), lambda i: (0, 0))],
        out_specs=pl.BlockSpec((tr, D), lambda i: (i, 0)),
        out_shape=jax.ShapeDtypeStruct((M, D), BF16),
        compiler_params=_params(("parallel",), 48),
        name="rmsnorm",
    )(x, g.reshape(1, D))


def _resnorm_kernel(y_ref, x_ref, gp_ref, gn_ref, xo_ref, ho_ref):
    xn = x_ref[...] + _rms(y_ref[...], gp_ref[...])
    xo_ref[...] = xn
    ho_ref[...] = _rms(xn, gn_ref[...]).astype(ho_ref.dtype)


def _resnorm_last_kernel(y_ref, x_ref, gp_ref, xo_ref):
    xo_ref[...] = x_ref[...] + _rms(y_ref[...], gp_ref[...])


def resnorm(y, x, g_post, g_next):
    M, D = x.shape
    tr = _pick_tile(M, 512, 16)
    row = pl.BlockSpec((tr, D), lambda i: (i, 0))
    vec = pl.BlockSpec((1, D), lambda i: (0, 0))
    if g_next is None:
        return pl.pallas_call(
            _resnorm_last_kernel, grid=(M // tr,), in_specs=[row, row, vec], out_specs=row,
            out_shape=jax.ShapeDtypeStruct((M, D), F32),
            compiler_params=_params(("parallel",), 48), name="resnorm_last",
        )(y, x, g_post.reshape(1, D)), None
    return pl.pallas_call(
        _resnorm_kernel, grid=(M // tr,), in_specs=[row, row, vec, vec], out_specs=[row, row],
        out_shape=[jax.ShapeDtypeStruct((M, D), F32), jax.ShapeDtypeStruct((M, D), BF16)],
        compiler_params=_params(("parallel",), 56), name="resnorm",
    )(y, x, g_post.reshape(1, D), g_next.reshape(1, D))


def _mm_kernel(a_ref, w_ref, o_ref):
    o_ref[...] = jnp.dot(a_ref[...], w_ref[...], preferred_element_type=F32).astype(o_ref.dtype)


def _mm_nt_kernel(a_ref, wt_ref, o_ref):
    o_ref[...] = lax.dot_general(a_ref[...], wt_ref[...], (((1,), (1,)), ((), ())),
                                 preferred_element_type=F32).astype(o_ref.dtype)


def matmul_nt(a, wt, out_dtype=F32, tm_target=1024, tn_target=512):
    M, K = a.shape
    N = wt.shape[0]
    tm = _pick_tile(M, tm_target, 16)
    tn = _pick_tile(N, tn_target, LANES)
    osz = jnp.dtype(out_dtype).itemsize
    vmem = 2 * (tm * K * 2 + K * tn * 2 + tm * tn * osz) + tm * tn * 4
    return pl.pallas_call(
        _mm_nt_kernel,
        grid=(M // tm, N // tn),
        in_specs=[pl.BlockSpec((tm, K), lambda i, j: (i, 0)), pl.BlockSpec((tn, K), lambda i, j: (j, 0))],
        out_specs=pl.BlockSpec((tm, tn), lambda i, j: (i, j)),
        out_shape=jax.ShapeDtypeStruct((M, N), out_dtype),
        compiler_params=_params(("parallel", "parallel"), vmem / MIB + 8),
        name="matmul_nt",
    )(a, wt)


def _mm_acc_kernel(a_ref, w_ref, o_ref, acc_ref):
    k = pl.program_id(2)

    @pl.when(k == 0)
    def _():
        acc_ref[...] = jnp.zeros_like(acc_ref)

    acc_ref[...] += jnp.dot(a_ref[...], w_ref[...], preferred_element_type=F32)

    @pl.when(k == pl.num_programs(2) - 1)
    def _():
        o_ref[...] = acc_ref[...].astype(o_ref.dtype)


def matmul(a, w, layer=None, out_dtype=F32, tm_target=1024, tn_target=512, tk_target=6144):
    M, K = a.shape
    N = w.shape[-1]
    tm = _pick_tile(M, tm_target, 16)
    tn = _pick_tile(N, tn_target, LANES)
    tk = _pick_tile(K, tk_target, LANES)
    nk = K // tk
    osz = jnp.dtype(out_dtype).itemsize
    vmem = 2 * (tm * tk * 2 + tk * tn * 2 + tm * tn * osz) + tm * tn * 4
    vmem_mib = vmem / MIB + 8
    if layer is None:
        w_spec2 = pl.BlockSpec((K, tn), lambda i, j: (0, j))
        w_spec3 = pl.BlockSpec((tk, tn), lambda i, j, k: (k, j))
    else:
        w_spec2 = pl.BlockSpec((None, K, tn), lambda i, j: (layer, 0, j))
        w_spec3 = pl.BlockSpec((None, tk, tn), lambda i, j, k: (layer, k, j))
    if nk == 1:
        return pl.pallas_call(
            _mm_kernel,
            grid=(M // tm, N // tn),
            in_specs=[pl.BlockSpec((tm, K), lambda i, j: (i, 0)), w_spec2],
            out_specs=pl.BlockSpec((tm, tn), lambda i, j: (i, j)),
            out_shape=jax.ShapeDtypeStruct((M, N), out_dtype),
            compiler_params=_params(("parallel", "parallel"), vmem_mib),
            name="matmul",
        )(a, w)
    return pl.pallas_call(
        _mm_acc_kernel,
        grid=(M // tm, N // tn, nk),
        in_specs=[pl.BlockSpec((tm, tk), lambda i, j, k: (i, k)), w_spec3],
        out_specs=pl.BlockSpec((tm, tn), lambda i, j, k: (i, j)),
        out_shape=jax.ShapeDtypeStruct((M, N), out_dtype),
        scratch_shapes=[pltpu.VMEM((tm, tn), F32)],
        compiler_params=_params(("parallel", "parallel", "arbitrary"), vmem_mib),
        name="matmul_acc",
    )(a, w)


def _merge_kernel(a1_ref, w1_ref, a2_ref, w2_ref, ga_ref, gb_ref, o_ref):
    f = jnp.dot(a1_ref[...], w1_ref[...], preferred_element_type=F32)
    r = jnp.dot(a2_ref[...], w2_ref[...], preferred_element_type=F32)
    o_ref[...] = (jax.nn.sigmoid(ga_ref[...]) * f + jax.nn.sigmoid(gb_ref[...]) * r).astype(o_ref.dtype)


def gated_merge(fox, w_of, rwkv, w_or, layer, proj, ga_off, gb_off):
    M, K1 = fox.shape
    K2 = rwkv.shape[1]
    D = w_of.shape[2]
    tm = _pick_tile(M, 1024, 16)
    tn = _pick_tile(D, 512, LANES)
    assert ga_off % tn == 0 and gb_off % tn == 0
    ja, jb = ga_off // tn, gb_off // tn
    vmem = 2 * (tm * (K1 + K2) * 2 + (K1 + K2) * tn * 2 + 2 * tm * tn * 4 + tm * tn * 2) + 2 * tm * tn * 4
    return pl.pallas_call(
        _merge_kernel,
        grid=(M // tm, D // tn),
        in_specs=[
            pl.BlockSpec((tm, K1), lambda i, j: (i, 0)), pl.BlockSpec((None, K1, tn), lambda i, j: (layer, 0, j)),
            pl.BlockSpec((tm, K2), lambda i, j: (i, 0)), pl.BlockSpec((None, K2, tn), lambda i, j: (layer, 0, j)),
            pl.BlockSpec((tm, tn), lambda i, j: (i, ja + j)), pl.BlockSpec((tm, tn), lambda i, j: (i, jb + j)),
        ],
        out_specs=pl.BlockSpec((tm, tn), lambda i, j: (i, j)),
        out_shape=jax.ShapeDtypeStruct((M, D), BF16),
        compiler_params=_params(("parallel", "parallel"), vmem / MIB + 8),
        name="gated_merge",
    )(fox, w_of, rwkv, w_or, proj, proj)


def _shift_rows(x, fill_rows):
    n = len(fill_rows)
    out = pltpu.roll(x, n, axis=0)
    row = lax.broadcasted_iota(jnp.int32, x.shape, 0)
    for i, f in enumerate(fill_rows):
        out = jnp.where(row == i, f, out)
    return out


def _rwkv_prep_kernel(r_ref, k_ref, v_ref, t_ref, rh_ref, kh_ref, vh_ref, th_ref,
                      r0_ref, k0_ref, v0_ref, t0_ref, mur_ref, muk_ref, muv_ref, mut_ref,
                      w0_ref, a0_ref, kk_ref, ka_ref, wup_ref, aup_ref, gup_ref,
                      ro_ref, ld_ref, ko_ref, vo_ref, ao_ref, bo_ref, go_ref):
    first = pl.program_id(1) == 0

    def tokshift(x_ref, h_ref, s_ref, mu_ref):
        x = x_ref[...]
        prev0 = jnp.where(first, s_ref[...], h_ref[SUBLANES - 1:SUBLANES, :])
        prev = _shift_rows(x, [prev0])
        return x + (prev - x) * mu_ref[...]

    r = tokshift(r_ref, rh_ref, r0_ref, mur_ref)
    k = tokshift(k_ref, kh_ref, k0_ref, muk_ref)
    v = tokshift(v_ref, vh_ref, v0_ref, muv_ref)
    t = tokshift(t_ref, th_ref, t0_ref, mut_ref)

    wlin = w0_ref[...] + jnp.dot(jnp.tanh(t).astype(BF16), wup_ref[...], preferred_element_type=F32)
    w = -jax.nn.softplus(-wlin) - 0.5
    a = jax.nn.sigmoid(a0_ref[...] + jnp.dot(t.astype(BF16), aup_ref[...], preferred_element_type=F32))
    g = jnp.dot(jax.nn.sigmoid(t).astype(BF16), gup_ref[...], preferred_element_type=F32)

    kkp = k * kk_ref[...]
    sq = kkp * kkp
    cw = sq.shape[1]
    row = lax.broadcasted_iota(jnp.int32, (LANES, LANES), 0)
    col = lax.broadcasted_iota(jnp.int32, (LANES, LANES), 1)
    ones_bd = jnp.where(row // CHUNK == col // CHUNK, 1.0, 0.0).astype(F32)
    parts = [jnp.dot(sq[:, c:c + LANES], ones_bd, precision=HIGHEST, preferred_element_type=F32)
             for c in range(0, cw, LANES)]
    nrm = jnp.sqrt(jnp.concatenate(parts, axis=1) if len(parts) > 1 else parts[0])
    kk = kkp / jnp.maximum(nrm, 1e-12)

    ro_ref[...] = r.astype(ro_ref.dtype)
    ld_ref[...] = -jnp.exp(w)
    ko_ref[...] = (k * (1.0 + (a - 1.0) * ka_ref[...])).astype(ko_ref.dtype)
    vo_ref[...] = v.astype(vo_ref.dtype)
    ao_ref[...] = (-kk).astype(ao_ref.dtype)
    bo_ref[...] = (kk * a).astype(bo_ref.dtype)
    go_ref[...] = g.astype(go_ref.dtype)


def rwkv_prep(proj, lay, grp, shift0, mu, w0, a0, k_k, k_a, wup, aup, gup):
    RW, TW = lay["RW"], lay["TW"]
    n_seq, T, row0 = grp
    R = _pick_tile(T, 1024, 16)
    cb = _pick_tile(RW, 512, LANES)
    nR, nC = T // R, RW // cb
    rb0 = row0 // R
    hb0 = row0 // SUBLANES
    assert row0 % R == 0 and lay["r_off"] % cb == 0 and lay["t_off"] % TW == 0
    cr, ck, cv = (lay[n] // cb for n in ("r_off", "rk_off", "rv_off"))
    ct = lay["t_off"] // TW

    def main(c0):
        return pl.BlockSpec((R, cb), lambda s, i, c: (rb0 + s * nR + i, c0 + c))

    def halo(c0):
        return pl.BlockSpec((SUBLANES, cb),
                            lambda s, i, c: (jnp.maximum(hb0 + (s * T + i * R) // SUBLANES - 1, 0), c0 + c))

    tail = pl.BlockSpec((R, TW), lambda s, i, c: (rb0 + s * nR + i, ct))
    tail_h = pl.BlockSpec((SUBLANES, TW),
                          lambda s, i, c: (jnp.maximum(hb0 + (s * T + i * R) // SUBLANES - 1, 0), ct))
    s0 = pl.BlockSpec((None, 1, cb), lambda s, i, c: (s, 0, c))
    s0t = pl.BlockSpec((None, 1, TW), lambda s, i, c: (s, 0, 0))
    vec = pl.BlockSpec((1, cb), lambda s, i, c: (0, c))
    vect = pl.BlockSpec((1, TW), lambda s, i, c: (0, 0))
    lora = pl.BlockSpec((TW, cb), lambda s, i, c: (0, c))
    out = pl.BlockSpec((R, cb), lambda s, i, c: (s * nR + i, c))
    in_specs = [main(cr), main(ck), main(cv), tail, halo(cr), halo(ck), halo(cv), tail_h,
                s0, s0, s0, s0t, vec, vec, vec, vect, vec, vec, vec, vec, lora, lora, lora]
    args = [proj, proj, proj, proj, proj, proj, proj, proj, *shift0, *mu, w0, a0, k_k, k_a, wup, aup, gup]
    return pl.pallas_call(
        _rwkv_prep_kernel,
        grid=(n_seq, nR, nC),
        in_specs=in_specs,
        out_specs=[out] * 7,
        out_shape=[jax.ShapeDtypeStruct((n_seq * T, RW), F32 if (i == 1 or R % 16) else BF16) for i in range(7)],
        compiler_params=_params(("parallel", "parallel", "parallel"), 48),
        name="rwkv_prep",
    )(*args)


def _seg_sum(x, lo):
    s0 = jnp.sum(jnp.where(lo, x, 0.0), axis=1, keepdims=True)
    s1 = jnp.sum(jnp.where(lo, 0.0, x), axis=1, keepdims=True)
    return jnp.where(lo, s0, s1)


def _rwkv_scan_kernel(r_ref, ld_ref, k_ref, v_ref, a_ref, b_ref, g_ref, s0_ref, rk_ref, gng_ref, gnb_ref,
                      o_ref, so_ref, st_ref, *, n_chunks, n0, rows, n_dbl, pairs):
    C = CHUNK
    row = lax.broadcasted_iota(jnp.int32, (C, LANES), 0)
    lane = lax.broadcasted_iota(jnp.int32, (C, LANES), 1)
    lane_in = lane % C
    lo = lane < C
    m_strict = lane_in < row
    m_incl = lane_in <= row
    eye_pair = jnp.where(lane_in == row, 1.0, 0.0).astype(F32)
    row2 = lax.broadcasted_iota(jnp.int32, (2 * C, LANES), 0)
    lane2 = lax.broadcasted_iota(jnp.int32, (2 * C, LANES), 1)
    bdm = (row2 // C) == (lane2 // C)
    eye2 = row2 == lane2
    tr = lax.broadcasted_iota(jnp.int32, (C, C), 0)
    tc = lax.broadcasted_iota(jnp.int32, (C, C), 1)
    tril = jnp.where(tc <= tr, 1.0, 0.0).astype(F32)

    def bd(x):
        return jnp.where(bdm, jnp.concatenate([x, x], axis=0), 0.0)

    def mm(x, y):
        return jnp.dot(x.astype(BF16), y.astype(BF16), preferred_element_type=F32)

    def mm_tn(x, y):
        return mm(x.T, y)

    def mm_nt(x, y):
        return lax.dot_general(x.astype(BF16), y.astype(BF16), (((1,), (1,)), ((), ())),
                               preferred_element_type=F32)

    for p in range(pairs):
        st_ref[p] = jnp.where(bdm, jnp.concatenate([s0_ref[p], s0_ref[p]], axis=1), 0.0)

    def chunk(c, carry):
        if n_chunks == 1:
            start = 0
            nact = n0
        else:
            start = pl.multiple_of(jnp.where(c == 0, 0, n0 + (c - 1) * C), 16)
            nact = jnp.where(c == 0, n0, C)
        act = row < nact

        lss = [slice(p * LANES, (p + 1) * LANES) for p in range(pairs)]

        def each(f, *cols):
            return [f(*xs) for xs in zip(*cols)]

        def load(ref):
            def one(ls):
                if rows >= C:
                    return ref[pl.ds(start, C), ls].astype(F32)
                x = ref[:, ls].astype(F32)
                return jnp.concatenate([x, jnp.zeros((C - rows, LANES), F32)], axis=0)
            return [one(ls) for ls in lss]

        def masked(xs):
            return [jnp.where(act, x, 0.0) for x in xs]

        r_raw, k_raw, v_raw = load(r_ref), load(k_ref), load(v_ref)
        logd = masked(load(ld_ref))
        r, k, v = masked(r_raw), masked(k_raw), masked(v_raw)
        a, b = masked(load(a_ref)), masked(load(b_ref))

        lw = each(lambda x: jnp.dot(tril, x, precision=HIGHEST, preferred_element_type=F32), logd)
        lw_end = each(lambda x: x[C - 1:C, :], lw)
        at = each(lambda x, l, d: x * jnp.exp(l - d), a, lw, logd)
        rt = each(lambda x, l: x * jnp.exp(l), r, lw)
        dec_out = each(lambda l: jnp.exp(-l), lw)
        bt = each(jnp.multiply, b, dec_out)
        kt = each(jnp.multiply, k, dec_out)
        dec_tail = each(lambda e, l: jnp.exp(e - l), lw_end, lw)
        btl = each(jnp.multiply, b, dec_tail)
        ktl = each(jnp.multiply, k, dec_tail)

        big = each(lambda x, y, z, w: mm_nt(jnp.concatenate([x, y], axis=0),
                                            jnp.concatenate([bd(z), bd(w)], axis=0)), at, rt, bt, kt)
        a_ab = each(lambda x: jnp.where(m_strict, x[0:C, 0:LANES], 0.0), big)
        a_ak = each(lambda x: jnp.where(m_strict, x[0:C, LANES:2 * LANES], 0.0), big)
        a_rb = each(lambda x: jnp.where(m_incl, x[C:2 * C, 0:LANES], 0.0), big)
        a_rk = each(lambda x: jnp.where(m_incl, x[C:2 * C, LANES:2 * LANES], 0.0), big)

        pw = a_ab
        x = each(lambda m: eye_pair + m, a_ab)
        for _ in range(n_dbl):
            pw = each(lambda m: mm(m, bd(m)), pw)
            x = each(lambda xm, m: xm + mm(xm, bd(m)), x, pw)

        av_op = each(lambda m, n, vv: mm(jnp.concatenate([m, n], axis=0), bd(vv)), a_ak, a_rk, v)
        z = each(lambda xm, m, n: mm(xm, jnp.concatenate([bd(m), bd(n[0:C])], axis=1)), x, at, av_op)
        y = each(lambda m, zz: mm(m, jnp.concatenate([bd(zz[:, 0:LANES]), bd(zz[:, LANES:2 * LANES])], axis=1)),
                 a_rb, z)
        gmat = each(lambda m, yy: m + yy[:, 0:LANES], rt, y)
        hmat = each(lambda m, yy: m[C:2 * C] + yy[:, LANES:2 * LANES], av_op, y)
        m_mat = each(lambda e, m, zz: jnp.where(eye2, jnp.broadcast_to(jnp.exp(e), (2 * C, LANES)), 0.0)
                     + jnp.where(bdm, mm_tn(m, zz[:, 0:LANES]), 0.0), lw_end, btl, z)
        n_mat = each(lambda m, n, zz, vv: jnp.where(
            bdm, mm_tn(jnp.concatenate([m, n], axis=0),
                       jnp.concatenate([zz[:, LANES:2 * LANES], vv], axis=0)), 0.0), btl, ktl, z, v)

        st = [st_ref[p] for p in range(pairs)]
        o = each(lambda gm, s, hm: mm(gm, s) + hm, gmat, st, hmat)
        st_new = each(lambda mmat, s, nm: mm(mmat, s) + nm, m_mat, st, n_mat)
        for p in range(pairs):
            st_ref[p] = st_new[p]

        oc = each(lambda m: m - _seg_sum(m, lo) * (1.0 / C), o)
        var = each(lambda m: _seg_sum(m * m, lo) * (1.0 / C), oc)
        on = each(lambda m, vv, ls: m * lax.rsqrt(vv + GN_EPS) * gng_ref[:, ls] + gnb_ref[:, ls], oc, var, lss)
        bonus = each(lambda rr, kk, vv, ls: _seg_sum(rr * kk * rk_ref[:, ls], lo) * vv, r_raw, k_raw, v_raw, lss)
        res = each(lambda m, bo, g: ((m + bo) * g).astype(o_ref.dtype), on, bonus, load(g_ref))
        for ls, rs in zip(lss, res):
            if rows >= C:
                o_ref[pl.ds(start, C), ls] = rs
            else:
                o_ref[:, ls] = rs[0:rows]
        return carry

    if n_chunks == 1:
        chunk(0, 0)
    else:
        lax.fori_loop(0, n_chunks, chunk, 0)
    for p in range(pairs):
        so_ref[p] = st_ref[p]


def rwkv_scan(prep, n_seq, T, s0t, r_k, gn_g, gn_b, n0, out_rows):
    RW = prep[0].shape[1]
    pairs = next(p for p in (8, 4, 2, 1) if (RW // LANES) % p == 0)
    cb = pairs * LANES
    nP = RW // cb
    if T >= CHUNK:
        assert (T - n0) % CHUNK == 0 and n0 % 16 == 0
        n_chunks = 1 + (T - n0) // CHUNK
        nmax = CHUNK
    else:
        assert T == n0
        n_chunks = 1
        nmax = n0
    n_dbl = max(0, (nmax - 1).bit_length() - 1)
    seq = pl.BlockSpec((T, cb), lambda s, p: (s, p))
    seq_in = pl.BlockSpec((T, cb), lambda s, p: (s, p), pipeline_mode=pl.Buffered(1))
    vec = pl.BlockSpec((1, cb), lambda s, p: (0, p))
    st_in = pl.BlockSpec((None, pairs, LANES, CHUNK), lambda s, p: (s, p, 0, 0))
    st_out = pl.BlockSpec((None, pairs, LANES, LANES), lambda s, p: (s, p, 0, 0))
    out_dtype = BF16 if T % 16 == 0 else F32
    return pl.pallas_call(
        functools.partial(_rwkv_scan_kernel, n_chunks=n_chunks, n0=n0, rows=T, n_dbl=n_dbl, pairs=pairs),
        grid=(n_seq, nP),
        in_specs=[seq_in] * 7 + [st_in, vec, vec, vec],
        out_specs=[seq, st_out],
        out_shape=[jax.ShapeDtypeStruct((out_rows, RW), out_dtype),
                   jax.ShapeDtypeStruct((n_seq, RW // LANES, LANES, LANES), F32)],
        scratch_shapes=[pltpu.VMEM((pairs, LANES, LANES), F32)],
        compiler_params=_params(("parallel", "parallel"), 56),
        name="rwkv_scan",
    )(*prep, s0t, r_k, gn_g, gn_b)


def _fox_lf_kernel(ff_ref, bias_ref, lf_ref, c_ref, *, tb):
    T = ff_ref.shape[0]
    z = ff_ref[...] + bias_ref[...]
    lf = jnp.minimum(z, 0.0) - jnp.log1p(jnp.exp(-jnp.abs(z)))
    lf_ref[...] = lf
    for r0 in range(0, T, tb):
        row = lax.broadcasted_iota(jnp.int32, (tb, T), 0) + r0
        col = lax.broadcasted_iota(jnp.int32, (tb, T), 1)
        tri = jnp.where(col <= row, 1.0, 0.0).astype(F32)
        c_ref[r0:r0 + tb, :] = jnp.dot(tri, lf, precision=HIGHEST, preferred_element_type=F32)


def fox_lf(ff, bias):
    n_seq, T, H = ff.shape
    tb = _pick_tile(T, 512, SUBLANES)
    blk = pl.BlockSpec((None, T, H), lambda s: (s, 0, 0))
    return pl.pallas_call(
        functools.partial(_fox_lf_kernel, tb=tb),
        grid=(n_seq,),
        in_specs=[blk, pl.BlockSpec((1, H), lambda s: (0, 0))],
        out_specs=[blk, blk],
        out_shape=[jax.ShapeDtypeStruct((n_seq, T, H), F32)] * 2,
        compiler_params=_params(("parallel",), 48),
        name="fox_lf",
    )(ff, bias.reshape(1, H))


def _fox_prompt_kernel(q_ref, k_ref, v_ref, c_ref, ct_ref, o_ref, *, tq, scale):
    T, Dh = q_ref.shape
    nb = T // tq
    H = c_ref.shape[1]
    h = pl.program_id(1)
    hl = lax.broadcasted_iota(jnp.int32, (T, H), 1)
    ccol = jnp.sum(jnp.where(hl == h, c_ref[...], 0.0), axis=1, keepdims=True)
    crow = ct_ref[h]

    for qi in range(nb):
        n_keys = (qi + 1) * tq
        q = q_ref[qi * tq:(qi + 1) * tq, :].astype(BF16)
        kb = k_ref[0:n_keys, :].astype(BF16)
        s = lax.dot_general(q, kb, (((1,), (1,)), ((), ())), preferred_element_type=F32) * scale
        s = s + (ccol[qi * tq:(qi + 1) * tq, :] - crow[:, 0:n_keys])
        ri = lax.broadcasted_iota(jnp.int32, (tq, n_keys), 0) + qi * tq
        ci = lax.broadcasted_iota(jnp.int32, (tq, n_keys), 1)
        s = jnp.where(ci <= ri, s, NEG_INF)
        p = jnp.exp(s - jnp.max(s, axis=1, keepdims=True))
        l = jnp.sum(p, axis=1, keepdims=True)
        acc = jnp.dot(p.astype(BF16), v_ref[0:n_keys, :].astype(BF16), preferred_element_type=F32)
        o_ref[qi * tq:(qi + 1) * tq, :] = (acc / l).astype(o_ref.dtype)


def fox_prompt(proj, lay, B, T, c, out_rows):
    FH, Dh = lay["FH"], lay["Dh"]
    tq = _pick_tile(T, 512, SUBLANES)
    ct = jnp.swapaxes(c, 1, 2)[:, :, None, :]
    assert Dh % LANES == 0 and lay["q_off"] % Dh == 0
    cq, ck, cv = (lay[n] // Dh for n in ("q_off", "k_off", "v_off"))
    return pl.pallas_call(
        functools.partial(_fox_prompt_kernel, tq=tq, scale=Dh ** -0.5),
        grid=(B, FH),
        in_specs=[
            pl.BlockSpec((T, Dh), lambda b, h: (b, cq + h)),
            pl.BlockSpec((T, Dh), lambda b, h: (b, ck + h)),
            pl.BlockSpec((T, Dh), lambda b, h: (b, cv + h)),
            pl.BlockSpec((None, T, FH), lambda b, h: (b, 0, 0)),
            pl.BlockSpec((None, FH, 1, T), lambda b, h: (b, 0, 0, 0)),
        ],
        out_specs=pl.BlockSpec((T, Dh), lambda b, h: (b, h)),
        out_shape=jax.ShapeDtypeStruct((out_rows, FH * Dh), BF16),
        compiler_params=_params(("parallel", "parallel"), 48),
        name="fox_prompt",
    )(proj, proj, proj, c, ct)


def _page_sums_kernel(lf_ref, suf_ref, tot_ref, *, FH):
    x = lf_ref[...]
    hi = x.astype(BF16)
    lo = (x - hi.astype(F32)).astype(BF16)
    lo2 = (x - hi.astype(F32) - lo.astype(F32)).astype(BF16)
    nv = x.shape[1] // LANES
    r = lax.broadcasted_iota(jnp.int32, (LANES, 2 * LANES), 0)
    c = lax.broadcasted_iota(jnp.int32, (LANES, 2 * LANES), 1)
    same_head = (r % FH) == (c % FH)
    later = (r // FH) > ((c % LANES) // FH)
    sel = jnp.where(same_head & (later | (c >= LANES)), 1.0, 0.0).astype(BF16)
    carry = jnp.zeros((x.shape[0], LANES), F32)
    for v in reversed(range(nv)):
        ls = slice(v * LANES, (v + 1) * LANES)
        y = (jnp.dot(hi[:, ls], sel, preferred_element_type=F32)
             + jnp.dot(lo[:, ls], sel, preferred_element_type=F32)
             + jnp.dot(lo2[:, ls], sel, preferred_element_type=F32))
        suf_ref[:, ls] = y[:, 0:LANES] + carry
        carry = carry + y[:, LANES:2 * LANES]
    for v in range(nv):
        tot_ref[:, v * LANES:(v + 1) * LANES] = carry


def fox_page_sums(lf_rows, FH):
    n, W = lf_rows.shape
    assert LANES % FH == 0 and W % LANES == 0
    tr = _pick_tile(n, 256, SUBLANES)
    blk = pl.BlockSpec((tr, W), lambda i: (i, 0))
    return pl.pallas_call(
        functools.partial(_page_sums_kernel, FH=FH),
        grid=(n // tr,),
        in_specs=[blk],
        out_specs=[blk, blk],
        out_shape=[jax.ShapeDtypeStruct((n, W), F32)] * 2,
        compiler_params=_params(("parallel",), 48),
        name="fox_page_sums",
    )(lf_rows)


def _fox_decode_kernel(pt_ref, q_ref, kn_ref, vn_ref, *refs, FH, scale, G):
    kp_refs, vp_refs, suf_refs, tot_refs = (refs[i * G:(i + 1) * G] for i in range(4))
    ccol_ref, crow_ref, o_ref, m_ref, l_ref, acc_ref, r_ref, mask_ref = refs[4 * G:]
    j = pl.program_id(1)
    nj = pl.num_programs(1)
    R = q_ref.shape[0]

    @pl.when(j == 0)
    def _():
        m_ref[...] = jnp.full_like(m_ref, NEG_INF)
        l_ref[...] = jnp.zeros_like(l_ref)
        acc_ref[...] = jnp.zeros_like(acc_ref)
        r_ref[...] = jnp.zeros_like(r_ref)
        rr = lax.broadcasted_iota(jnp.int32, mask_ref.shape, 0)
        cc = lax.broadcasted_iota(jnp.int32, mask_ref.shape, 1)
        mask_ref[...] = jnp.where((rr % FH) == (cc % FH), 0.0, NEG_INF)

    q = q_ref[...].astype(BF16)
    ccol = ccol_ref[...]

    def update(ss, vbs):
        m_old = m_ref[...]
        m_new = m_old
        for s in ss:
            m_new = jnp.maximum(m_new, jnp.max(s, axis=1, keepdims=True))
        alpha = jnp.exp(m_old - m_new)
        ps = [jnp.exp(s - m_new) for s in ss]
        l_new = alpha * l_ref[...]
        acc = alpha * acc_ref[...]
        for p, vb in zip(ps, vbs):
            l_new = l_new + jnp.sum(p, axis=1, keepdims=True)
            acc = acc + jnp.dot(p.astype(BF16), vb, preferred_element_type=F32)
        l_ref[...] = l_new
        acc_ref[...] = acc
        m_ref[...] = m_new

    def scores(k_ref):
        return lax.dot_general(q, k_ref[...].astype(BF16), (((1,), (1,)), ((), ())),
                               preferred_element_type=F32) * scale

    ss = [scores(k_ref) for k_ref in kp_refs]
    later = r_ref[...]
    for g in range(G):
        ss[g] = ss[g] + (mask_ref[...] + (suf_refs[g][...] + later)) + ccol
        later = later + tot_refs[g][...]
    update(ss, [v_ref[...].astype(BF16) for v_ref in vp_refs])
    r_ref[...] = later

    @pl.when(j == nj - 1)
    def _():
        ri = lax.broadcasted_iota(jnp.int32, (R, R), 0)
        ci = lax.broadcasted_iota(jnp.int32, (R, R), 1)
        ok = ((ri % FH) == (ci % FH)) & ((ci // FH) <= (ri // FH))
        sn = jnp.where(ok, scores(kn_ref) + (ccol - crow_ref[...]), NEG_INF)
        update([sn], [vn_ref[...].astype(BF16)])
        o_ref[...] = (acc_ref[...] / l_ref[...]).astype(o_ref.dtype)


def fox_decode(q2, kn2, vn2, cache_k2, cache_v2, suf, tot, page_table, layer, c_new, FH):
    DB, R, Dh = q2.shape
    NP = page_table.shape[1]
    DEPTH, W = cache_k2.shape[1], cache_k2.shape[2]
    ccol = c_new.reshape(DB, R, 1)
    crow = c_new.reshape(DB, 1, R)
    pt = page_table.reshape(-1)
    G = next(g for g in (4, 2, 1) if NP % g == 0)
    new_spec = pl.BlockSpec((None, R, Dh), lambda b, j, pt: (b, 0, 0))

    def page(b, j, pt, g):
        return pt[b * NP + NP - 1 - (j * G + g)]

    page_specs = [pl.BlockSpec((None, None, W, Dh), lambda b, j, pt, g=g: (page(b, j, pt, g), layer, 0, 0))
                  for g in range(G)]
    row_specs = [pl.BlockSpec((None, 1, W), lambda b, j, pt, g=g: (page(b, j, pt, g) * DEPTH + layer, 0, 0))
                 for g in range(G)]
    grid_spec = pltpu.PrefetchScalarGridSpec(
        num_scalar_prefetch=1,
        grid=(DB, NP // G),
        in_specs=[new_spec, new_spec, new_spec, *page_specs, *page_specs, *row_specs, *row_specs,
                  pl.BlockSpec((None, R, 1), lambda b, j, pt: (b, 0, 0)),
                  pl.BlockSpec((None, 1, R), lambda b, j, pt: (b, 0, 0))],
        out_specs=pl.BlockSpec((None, R, Dh), lambda b, j, pt: (b, 0, 0)),
        scratch_shapes=[pltpu.VMEM((R, 1), F32), pltpu.VMEM((R, 1), F32), pltpu.VMEM((R, Dh), F32),
                        pltpu.VMEM((1, W), F32), pltpu.VMEM((R, W), F32)],
    )
    return pl.pallas_call(
        functools.partial(_fox_decode_kernel, FH=FH, scale=Dh ** -0.5, G=G),
        grid_spec=grid_spec,
        out_shape=jax.ShapeDtypeStruct((DB, R, Dh), F32),
        compiler_params=_params(("parallel", "arbitrary"), 56),
        name="fox_decode",
    )(pt, q2, kn2, vn2, *([cache_k2] * G), *([cache_v2] * G), *([suf] * G), *([tot] * G), ccol, crow)


def _glu(u, up, p2, p1, cw_ref, cb_ref):
    u1 = _shift_rows(u, [p1])
    u2 = _shift_rows(u, [p2, p1])
    c = cb_ref[...] + cw_ref[0:1, :] * u2 + cw_ref[1:2, :] * u1 + cw_ref[2:3, :] * u
    act = 0.5 * c * (1.0 + jnp.tanh(0.7978845608028654 * (c + 0.044715 * c * c * c)))
    return act * up


def _convglu_kernel(u_ref, uh_ref, up_ref, c0_ref, cw_ref, cb_ref, o_ref):
    first = pl.program_id(1) == 0
    p2 = jnp.where(first, c0_ref[0:1, :], uh_ref[SUBLANES - 2:SUBLANES - 1, :])
    p1 = jnp.where(first, c0_ref[1:2, :], uh_ref[SUBLANES - 1:SUBLANES, :])
    o_ref[...] = _glu(u_ref[...], up_ref[...], p2, p1, cw_ref, cb_ref).astype(o_ref.dtype)


def _ffn_glu_kernel(a_ref, ah_ref, wg_ref, wu_ref, c0_ref, cw_ref, cb_ref, o_ref, ul_ref, *, tiles_per_seq):
    first = (pl.program_id(0) % tiles_per_seq) == 0
    a = a_ref[...]
    wg = wg_ref[...]
    u = jnp.dot(a, wg, preferred_element_type=F32)
    uh = jnp.dot(ah_ref[...], wg, preferred_element_type=F32)
    up = jnp.dot(a, wu_ref[...], preferred_element_type=F32)
    nh, tm = uh.shape[0], u.shape[0]
    p2 = jnp.where(first, c0_ref[0:1, :], uh[nh - 2:nh - 1, :])
    p1 = jnp.where(first, c0_ref[1:2, :], uh[nh - 1:nh, :])
    o_ref[...] = _glu(u, up, p2, p1, cw_ref, cb_ref).astype(o_ref.dtype)
    ul_ref[...] = u[tm - SUBLANES:tm, :]


def ffn_glu(h, w_gate, w_up, layer, n_seq, T, conv0, conv_w, conv_b, out_rows):
    K = h.shape[1]
    F = w_gate.shape[2]
    tm = _pick_tile(T, 1024, 16)
    tn = _pick_tile(F, 256, LANES)
    HB = 16
    tps = T // tm
    n_tiles = n_seq * tps
    w_spec = pl.BlockSpec((None, K, tn), lambda i, j: (layer, 0, j))
    vmem = 2 * (tm * K * 2 + HB * K * 2 + 2 * K * tn * 2 + tm * tn * 2) + 4 * tm * tn * 4
    return pl.pallas_call(
        functools.partial(_ffn_glu_kernel, tiles_per_seq=tps),
        grid=(n_tiles, F // tn),
        in_specs=[pl.BlockSpec((tm, K), lambda i, j: (i, 0)),
                  pl.BlockSpec((HB, K), lambda i, j: (jnp.maximum(i * (tm // HB) - 1, 0), 0)),
                  w_spec, w_spec,
                  pl.BlockSpec((None, 2, tn), lambda i, j: (i // tps, 0, j)),
                  pl.BlockSpec((3, tn), lambda i, j: (0, j)),
                  pl.BlockSpec((1, tn), lambda i, j: (0, j))],
        out_specs=[pl.BlockSpec((tm, tn), lambda i, j: (i, j)),
                   pl.BlockSpec((None, SUBLANES, tn), lambda i, j: (i, 0, j))],
        out_shape=[jax.ShapeDtypeStruct((out_rows, F), BF16),
                   jax.ShapeDtypeStruct((n_tiles, SUBLANES, F), F32)],
        compiler_params=_params(("parallel", "parallel"), vmem / MIB + 8),
        name="ffn_glu",
    )(h, h, w_gate, w_up, conv0, conv_w, conv_b.reshape(1, F))


def convglu(u, up, grp, conv0, conv_w, conv_b, out_rows):
    F = u.shape[1]
    n_seq, T, row0 = grp
    R = _pick_tile(T, 2064, SUBLANES)
    cb = _pick_tile(F, max(256, (2 * MIB) // (4 * R)), LANES)
    nR, nC = T // R, F // cb
    assert row0 % R == 0 and conv_w.shape[0] == 3
    rb0, hb0 = row0 // R, row0 // SUBLANES
    main = pl.BlockSpec((R, cb), lambda s, i, c: (rb0 + s * nR + i, c))
    halo = pl.BlockSpec((SUBLANES, cb),
                        lambda s, i, c: (jnp.maximum(hb0 + (s * T + i * R) // SUBLANES - 1, 0), c))
    in_specs = [main, halo, main,
                pl.BlockSpec((None, 2, cb), lambda s, i, c: (s, 0, c)),
                pl.BlockSpec((3, cb), lambda s, i, c: (0, c)),
                pl.BlockSpec((1, cb), lambda s, i, c: (0, c))]
    return pl.pallas_call(
        _convglu_kernel,
        grid=(n_seq, nR, nC),
        in_specs=in_specs,
        out_specs=pl.BlockSpec((R, cb), lambda s, i, c: (s * nR + i, c)),
        out_shape=jax.ShapeDtypeStruct((out_rows, F), BF16 if T % 16 == 0 else F32),
        compiler_params=_params(("parallel", "parallel", "parallel"), 48),
        name="convglu",
    )(u, u, up, conv0, conv_w, conv_b.reshape(1, F))


def _proj_layout(D, FH, Dh, RH, RN, DL, AL, GL):
    FW, RW = FH * Dh, RH * RN
    lay = {"FH": FH, "Dh": Dh, "RW": RW, "FW": FW}
    lay["q_off"], lay["k_off"], lay["v_off"] = 0, FW, 2 * FW
    lay["r_off"], lay["rk_off"], lay["rv_off"] = 3 * FW, 3 * FW + RW, 3 * FW + 2 * RW
    lay["ga_off"] = 3 * FW + 3 * RW
    lay["gb_off"] = lay["ga_off"] + D
    TW = _round_up(GL + DL + AL + FH, LANES)
    lay["TW"] = TW
    lay["t_off"] = _round_up(lay["gb_off"] + D, TW)
    lay["gd"], lay["wd"], lay["ad"], lay["ff"] = 0, GL, GL + DL, GL + DL + AL
    lay["NPJ"] = _round_up(lay["t_off"] + TW, 512)
    return lay


def _permute_proj(w, lay, D, DL, AL, GL, axis):
    FW, RW, FH = lay["FW"], lay["RW"], lay["FH"]
    axis = axis % w.ndim
    fox_end = 3 * FW + FH

    def cut(a, b):
        return lax.slice_in_dim(w, a, b, axis=axis)

    def zeros(n):
        return jnp.zeros(w.shape[:axis] + (n,) + w.shape[axis + 1:], w.dtype)

    rkv = cut(fox_end, fox_end + 3 * RW)
    wd = cut(fox_end + 3 * RW, fox_end + 3 * RW + DL)
    ad = cut(fox_end + 3 * RW + DL, fox_end + 3 * RW + DL + AL)
    gd = cut(fox_end + 3 * RW + DL + AL, fox_end + 3 * RW + DL + AL + GL)
    gates = cut(fox_end + 3 * RW + DL + AL + GL, w.shape[axis])
    ff = cut(3 * FW, fox_end)
    pad1 = zeros(lay["t_off"] - (lay["gb_off"] + D))
    pad2 = zeros(lay["NPJ"] - lay["t_off"] - (GL + DL + AL + FH))
    return jnp.concatenate([cut(0, 3 * FW), rkv, gates, pad1, gd, wd, ad, ff, pad2], axis=axis)


def _pad_lora(w_up, off, TW):
    L, RW = w_up.shape
    return jnp.zeros((TW, RW), BF16).at[off:off + L].set(w_up.astype(BF16))


def kernel(x_prompt, x_sample, cache_k, cache_v, cache_logf, state_wkv, state_shift, state_conv, page_table,
           meta_tokens, norm_mix_pre, norm_mix_post, norm_ffn_pre, norm_ffn_post, w_in, fox_forget_bias,
           rwkv_mu, rwkv_w0, rwkv_w_up, rwkv_a0, rwkv_a_up, rwkv_g_up, rwkv_k_k, rwkv_k_a, rwkv_r_k,
           rwkv_gn_g, rwkv_gn_b, w_o_fox, w_o_rwkv, w_out, ffn_w_gate, ffn_w_up, ffn_conv_w, ffn_conv_b,
           ffn_w_down):
    B, SEQ, D = x_prompt.shape
    DB, TS, _ = x_sample.shape
    n_pool, DEPTH, PAGE, FH, Dh = cache_k.shape
    NM = meta_tokens.shape[0]
    T = NM + SEQ
    RH, RN = rwkv_r_k.shape[1:]
    DL, AL, GL = rwkv_w_up.shape[1], rwkv_a_up.shape[1], rwkv_g_up.shape[1]
    F = ffn_w_gate.shape[2]
    CW = ffn_conv_w.shape[1]
    FW, RW = FH * Dh, RH * RN
    RP = 3 * RW + DL + AL + GL
    assert RN == CHUNK and FW == RW and CW == 3 and T % 16 == 0
    MP = B * T
    M = MP + DB * TS
    lay = _proj_layout(D, FH, Dh, RH, RN, DL, AL, GL)
    TW = lay["TW"]
    grp_p = (B, T, 0)
    grp_s = (DB, TS, MP)

    meta = jnp.broadcast_to(meta_tokens[None].astype(x_prompt.dtype), (B, NM, D))
    x = jnp.concatenate([jnp.concatenate([meta, x_prompt], axis=1).reshape(MP, D),
                         x_sample.reshape(DB * TS, D)], axis=0)

    cache_k2 = cache_k.reshape(n_pool, DEPTH, PAGE * FH, Dh)
    cache_v2 = cache_v.reshape(n_pool, DEPTH, PAGE * FH, Dh)
    suf, tot = fox_page_sums(cache_logf.reshape(n_pool * DEPTH, PAGE * FH), FH)
    suf = suf.reshape(n_pool * DEPTH, 1, PAGE * FH)
    tot = tot.reshape(n_pool * DEPTH, 1, PAGE * FH)

    def split_rows(a):
        return a[:MP].reshape(B, T, -1), a[MP:].reshape(DB, TS, -1)

    def join_rows(big, small):
        return lax.dynamic_update_slice(big, small.astype(big.dtype), (MP, 0))

    def orig_rwkv_cols(p):
        t = p[..., lay["t_off"]:]
        return jnp.concatenate([p[..., lay["r_off"]:lay["r_off"] + 3 * RW], t[..., lay["wd"]:lay["wd"] + DL],
                                t[..., lay["ad"]:lay["ad"] + AL], t[..., lay["gd"]:lay["gd"] + GL]], axis=-1)

    def rwkv_cols_split(v):
        tail = jnp.zeros(v.shape[:-1] + (TW,), v.dtype)
        tail = tail.at[..., lay["gd"]:lay["gd"] + GL].set(v[..., 3 * RW + DL + AL:])
        tail = tail.at[..., lay["wd"]:lay["wd"] + DL].set(v[..., 3 * RW:3 * RW + DL])
        tail = tail.at[..., lay["ad"]:lay["ad"] + AL].set(v[..., 3 * RW + DL:3 * RW + DL + AL])
        return v[..., :RW], v[..., RW:2 * RW], v[..., 2 * RW:3 * RW], tail

    w_in_t = jnp.swapaxes(w_in, 1, 2)
    w_of, w_or, w_o = w_o_fox.astype(BF16), w_o_rwkv.astype(BF16), w_out.astype(BF16)
    w_gate, w_up, w_down = ffn_w_gate.astype(BF16), ffn_w_up.astype(BF16), ffn_w_down.astype(BF16)

    h = rmsnorm_bf16(x, norm_mix_pre[0])
    rows_p, rows_s = [], []
    for l in range(DEPTH):
        wt_l = _permute_proj(w_in_t[l], lay, D, DL, AL, GL, 0).astype(BF16)
        proj = matmul_nt(h, wt_l)

        ff_p, ff_s = split_rows(proj[:, lay["t_off"] + lay["ff"]:lay["t_off"] + lay["ff"] + FH])
        lf_p, c_p = fox_lf(ff_p, fox_forget_bias[l])
        lf_s, c_s = fox_lf(ff_s, fox_forget_bias[l])
        q2, kn2, vn2 = (proj[MP:, lay[n]:lay[n] + FW].reshape(DB, TS * FH, Dh) for n in ("q_off", "k_off", "v_off"))
        fox_s = fox_decode(q2, kn2, vn2, cache_k2, cache_v2, suf, tot, page_table, l, c_s, FH)
        fox = join_rows(fox_prompt(proj, lay, B, T, c_p, M), fox_s.reshape(DB * TS, FW))

        mu = [m.reshape(1, -1) for m in rwkv_cols_split(rwkv_mu[l])]
        lora = (_pad_lora(rwkv_w_up[l], lay["wd"], TW), _pad_lora(rwkv_a_up[l], lay["ad"], TW),
                _pad_lora(rwkv_g_up[l], lay["gd"], TW))
        vecs = [v.reshape(1, RW) for v in (rwkv_w0[l], rwkv_a0[l], rwkv_k_k[l], rwkv_k_a[l])]
        sh_p = [s[:, None, :] for s in rwkv_cols_split(jnp.zeros((B, RP), F32))]
        sh_s = [s[:, None, :] for s in rwkv_cols_split(state_shift[:, l])]
        prep_p = rwkv_prep(proj, lay, grp_p, sh_p, mu, *vecs, *lora)
        prep_s = rwkv_prep(proj, lay, grp_s, sh_s, mu, *vecs, *lora)
        s0_p = jnp.zeros((B, RW // LANES, LANES, CHUNK), F32)
        s0_s = jnp.swapaxes(state_wkv[:, l], 2, 3).reshape(DB, RW // LANES, LANES, CHUNK)
        scan_vecs = (rwkv_r_k[l].reshape(1, RW), rwkv_gn_g[l].reshape(1, RW), rwkv_gn_b[l].reshape(1, RW))
        rwkv_p, st_p = rwkv_scan(prep_p, B, T, s0_p, *scan_vecs, NM, M)
        rwkv_s, st_s = rwkv_scan(prep_s, DB, TS, s0_s, *scan_vecs, TS, DB * TS)
        rwkv = join_rows(rwkv_p, rwkv_s)

        def unpack_state(st):
            n = st.shape[0]
            st = st.reshape(n, RW // LANES, 2, CHUNK, 2, CHUNK)
            d = jnp.stack([st[:, :, 0, :, 0, :], st[:, :, 1, :, 1, :]], axis=2)
            return jnp.swapaxes(d.reshape(n, RH, CHUNK, CHUNK), 2, 3)

        mixed = gated_merge(fox, w_of, rwkv, w_or, l, proj, lay["ga_off"], lay["gb_off"])
        y = matmul(mixed, w_o, l)
        x, h2 = resnorm(y, x, norm_mix_post[l], norm_ffn_pre[l])

        act_p, u_last = ffn_glu(h2, w_gate, w_up, l, B, T, jnp.zeros((B, CW - 1, F), F32),
                                ffn_conv_w[l], ffn_conv_b[l], M)
        h2_s = h2[MP:]
        u_s = matmul(h2_s, w_gate, l)
        up_s = matmul(h2_s, w_up, l)
        act_s = convglu(u_s, up_s, (DB, TS, 0), state_conv[:, l], ffn_conv_w[l], ffn_conv_b[l], DB * TS)
        f = matmul(join_rows(act_p, act_s), w_down, l)
        x, h = resnorm(f, x, norm_ffn_post[l], norm_mix_pre[l + 1] if l + 1 < DEPTH else None)

        k_p, k_s = split_rows(proj[:, lay["k_off"]:lay["k_off"] + FW])
        v_p, v_s = split_rows(proj[:, lay["v_off"]:lay["v_off"] + FW])
        tps = u_last.shape[0] // B
        conv_p = u_last[tps - 1::tps, SUBLANES - (CW - 1):]
        conv_s = u_s.reshape(DB, TS, F)[:, TS - (CW - 1):]
        sh_new_p = orig_rwkv_cols(proj[T - 1:MP:T])
        sh_new_s = orig_rwkv_cols(proj[MP + TS - 1::TS])
        rows_p.append((k_p.reshape(B, T, FH, Dh), v_p.reshape(B, T, FH, Dh), lf_p, unpack_state(st_p),
                       sh_new_p, conv_p))
        rows_s.append((k_s.reshape(DB, TS, FH, Dh), v_s.reshape(DB, TS, FH, Dh), lf_s, unpack_state(st_s),
                       sh_new_s, conv_s))

    def stack(rows, i):
        return jnp.stack([r[i] for r in rows], axis=1)

    y_prompt = x[:MP].reshape(B, T, D)[:, NM:]
    y_sample = x[MP:].reshape(DB, TS, D)
    return (y_prompt, y_sample,
            *(stack(rows_p, i) for i in range(6)),
            *(stack(rows_s, i) for i in range(6)))
```
